```python
import jax, jax.numpy as jnp
from jax import lax
import numpy as np

D_MODEL = 1024
BATCH = 4
SEQ = 4096
DEPTH = 4
DEC_BATCH = 16
DEC_SEQ = 64
PAST_LEN = 2048

CHUNK = 64
QBLK = 128
D_MIX = D_MODEL
N_MIXERS = 4
G_WIDTH = D_MIX // N_MIXERS
HEAD_DIM = 64
N_HEADS = G_WIDTH // HEAD_DIM
CONV_W = 31
N_MEM = 256
MEM_HEADS = 4
MEM_HEAD_DIM = D_MODEL // MEM_HEADS
D_FF = 4 * D_MODEL
EPS = 1e-6
NEG_BIG = -1e30
SPLIT_SIZES = (G_WIDTH, G_WIDTH, G_WIDTH, G_WIDTH, G_WIDTH, N_HEADS, G_WIDTH, G_WIDTH, G_WIDTH, G_WIDTH, G_WIDTH, G_WIDTH, G_WIDTH)
P_IN = 12 * G_WIDTH + N_HEADS

kernel_name = 'hybrid_streaming_encoder_step'


def _rmsnorm(x, g):
    xf = x.astype(jnp.float32)
    y = xf * lax.rsqrt(jnp.mean(xf * xf, axis=-1, keepdims=True) + EPS)
    return (y * g.astype(jnp.float32)).astype(x.dtype)


def _heads(t):
    return t.reshape(t.shape[:-1] + (N_HEADS, HEAD_DIM))


def _flat(t):
    return t.reshape(t.shape[:2] + (G_WIDTH,))


def _blocks(a, n):
    b, s = a.shape[0], a.shape[1]
    return jnp.moveaxis(a.reshape((b, s // n, n) + a.shape[2:]), 1, 0)


def _unblocks(a):
    a = jnp.moveaxis(a, 0, 1)
    return a.reshape((a.shape[0], a.shape[1] * a.shape[2]) + a.shape[3:])


def _conv_module(a, g, hist, w, bias, ln_g, ln_b):
    u = a * jax.nn.sigmoid(g)
    ext = jnp.concatenate([hist.astype(u.dtype), u], axis=1)
    y = lax.conv_general_dilated(ext, w[:, None, :].astype(u.dtype), (1,), 'VALID',
                                 dimension_numbers=('NWC', 'WIO', 'NWC'),
                                 feature_group_count=G_WIDTH)
    yf = y.astype(jnp.float32) + bias.astype(jnp.float32)
    mu = jnp.mean(yf, axis=-1, keepdims=True)
    var = jnp.mean(jnp.square(yf - mu), axis=-1, keepdims=True)
    yn = (yf - mu) * lax.rsqrt(var + EPS) * ln_g.astype(jnp.float32) + ln_b.astype(jnp.float32)
    return jax.nn.silu(yn).astype(a.dtype), ext[:, -(CONV_W - 1):]


def _fox_attend(q, cq, qpos, k, v, ck, kpos):
    s = jnp.einsum('bqhd,bkhd->bhqk', q, k).astype(jnp.float32) * (HEAD_DIM ** -0.5)
    s = s + jnp.swapaxes(cq, 1, 2)[..., :, None] - jnp.swapaxes(ck, 1, 2)[..., None, :]
    s = jnp.where(kpos[None, :] <= qpos[:, None], s, NEG_BIG)
    p = jax.nn.softmax(s, axis=-1)
    return jnp.einsum('bhqk,bkhd->bqhd', p.astype(v.dtype), v)


def _sb_attend(q, qpos, k, v, kpos):
    z = jnp.einsum('bqhd,bkhd->bhqk', q, k).astype(jnp.float32) * (HEAD_DIM ** -0.5)
    m = kpos[None, :] < qpos[:, None]
    u = jnp.where(m, jax.nn.log_sigmoid(-z), 0.0)
    rest = lax.cumsum(u, axis=3, reverse=True) - u
    a = jnp.where(m, jnp.exp(jnp.where(m, jax.nn.log_sigmoid(z) + rest, 0.0)), 0.0)
    return jnp.einsum('bhqk,bkhd->bqhd', a.astype(v.dtype), v)


def _hgrn_chunk(s0, q, k, v, logf):
    b = jnp.cumsum(logf, axis=1)
    n = q.shape[1]
    m = (jnp.arange(n)[:, None] >= jnp.arange(n)[None, :])[None, :, :, None, None]
    decay = jnp.where(m, jnp.exp(jnp.minimum(b[:, :, None] - b[:, None, :], 0.0)), 0.0)
    qf = q.astype(jnp.float32)
    vf = v.astype(jnp.float32)
    scores = jnp.einsum('bthk,bshk,btshk->bhts', qf, k, decay)
    o = jnp.einsum('bthk,bhkv->bthv', qf * jnp.exp(b), s0) + jnp.einsum('bhts,bshv->bthv', scores, vf)
    b_last = b[:, -1]
    s_new = jnp.exp(b_last)[..., None] * s0 + jnp.einsum('bshk,bshv->bhkv', k * jnp.exp(b_last[:, None] - b), vf)
    return s_new, o


def _mem_attend(h, mk, mv, wq, wo):
    q = (h @ wq).reshape(h.shape[:2] + (MEM_HEADS, MEM_HEAD_DIM))
    s = jnp.einsum('bqhd,bkhd->bhqk', q, mk.astype(q.dtype)).astype(jnp.float32) * (MEM_HEAD_DIM ** -0.5)
    p = jax.nn.softmax(s, axis=-1)
    o = jnp.einsum('bhqk,bkhd->bqhd', p.astype(mv.dtype), mv)
    return o.reshape(h.shape[:2] + (MEM_HEADS * MEM_HEAD_DIM,)) @ wo


def setup_inputs(seed: int = 0) -> dict:
    key = jax.random.key(seed)
    ks = jax.random.split(key, 32)

    def nrm(i, shape, scale=1.0):
        return scale * jax.random.normal(ks[i], shape, jnp.float32)

    H, HD, G = N_HEADS, HEAD_DIM, G_WIDTH
    return {
        'x_prompt': nrm(0, (BATCH, SEQ, D_MODEL)),
        'x_sample': nrm(1, (DEC_BATCH, DEC_SEQ, D_MODEL)),
        'mem_prompt': nrm(2, (BATCH, N_MEM, D_MODEL)),
        'cache_conv': nrm(3, (DEPTH, DEC_BATCH, CONV_W - 1, G), 0.5),
        'cache_fox_k': nrm(4, (DEPTH, DEC_BATCH, PAST_LEN, H, HD)),
        'cache_fox_v': nrm(5, (DEPTH, DEC_BATCH, PAST_LEN, H, HD)),
        'cache_fox_logf': jax.nn.log_sigmoid(2.5 + nrm(6, (DEPTH, DEC_BATCH, PAST_LEN, H), 0.5)),
        'state_hgrn': nrm(7, (DEPTH, DEC_BATCH, H, HD, HD), 0.3),
        'cache_sb_k': nrm(8, (DEPTH, DEC_BATCH, PAST_LEN, H, HD)),
        'cache_sb_v': nrm(9, (DEPTH, DEC_BATCH, PAST_LEN, H, HD)),
        'cache_mem_k': nrm(10, (DEPTH, DEC_BATCH, N_MEM, MEM_HEADS, MEM_HEAD_DIM)),
        'cache_mem_v': nrm(11, (DEPTH, DEC_BATCH, N_MEM, MEM_HEADS, MEM_HEAD_DIM)),
        'norm_mix': 1.0 + nrm(12, (DEPTH, D_MODEL), 0.02),
        'w_in': nrm(13, (DEPTH, D_MODEL, P_IN), D_MODEL ** -0.5),
        'b_fox_f': 2.5 + nrm(14, (DEPTH, H), 0.5),
        'conv_w': nrm(15, (DEPTH, CONV_W, G), CONV_W ** -0.5),
        'conv_b': nrm(16, (DEPTH, G), 0.02),
        'conv_ln_g': 1.0 + nrm(17, (DEPTH, G), 0.02),
        'conv_ln_b': nrm(18, (DEPTH, G), 0.02),
        'hgrn_lb': nrm(19, (DEPTH, G), 0.5),
        'hgrn_norm': 1.0 + nrm(20, (DEPTH, HD), 0.02),
        'w_out': nrm(21, (DEPTH, D_MIX, D_MODEL), D_MIX ** -0.5),
        'norm_mem': 1.0 + nrm(22, (DEPTH, D_MODEL), 0.02),
        'w_mq': nrm(23, (DEPTH, D_MODEL, MEM_HEADS * MEM_HEAD_DIM), D_MODEL ** -0.5),
        'w_mk': nrm(24, (DEPTH, D_MODEL, MEM_HEADS * MEM_HEAD_DIM), D_MODEL ** -0.5),
        'w_mv': nrm(25, (DEPTH, D_MODEL, MEM_HEADS * MEM_HEAD_DIM), D_MODEL ** -0.5),
        'w_mo': nrm(26, (DEPTH, MEM_HEADS * MEM_HEAD_DIM, D_MODEL), (MEM_HEADS * MEM_HEAD_DIM) ** -0.5),
        'norm_ffn': 1.0 + nrm(27, (DEPTH, D_MODEL), 0.02),
        'w_up': nrm(28, (DEPTH, D_MODEL, D_FF), D_MODEL ** -0.5),
        'w_down': nrm(29, (DEPTH, D_FF, D_MODEL), D_FF ** -0.5),
        'norm_final': 1.0 + nrm(30, (D_MODEL,), 0.02),
    }


def reference(x_prompt, x_sample, mem_prompt, cache_conv, cache_fox_k, cache_fox_v, cache_fox_logf,
              state_hgrn, cache_sb_k, cache_sb_v, cache_mem_k, cache_mem_v,
              norm_mix, w_in, b_fox_f, conv_w, conv_b, conv_ln_g, conv_ln_b, hgrn_lb, hgrn_norm,
              w_out, norm_mem, w_mq, w_mk, w_mv, w_mo, norm_ffn, w_up, w_down, norm_final):
    f32 = jnp.float32
    split_idx = np.cumsum(SPLIT_SIZES)[:-1].tolist()
    sm = jax.nn.softmax(hgrn_lb.astype(f32), axis=0)
    lower = jnp.clip(jnp.cumsum(sm, axis=0) - sm[0], 0.0, 1.0 - 1e-6).reshape(DEPTH, N_HEADS, HEAD_DIM)

    def project(x, l):
        p = _rmsnorm(x, norm_mix[l]) @ w_in[l]
        cv_a, cv_g, fq, fk, fv, ff, hq, hf, hi, hgate, sq, sk, sv = jnp.split(p, split_idx, axis=-1)
        f_logf = jax.nn.log_sigmoid((ff + b_fox_f[l]).astype(f32))
        hz = _heads(hf).astype(f32)
        lb = lower[l]
        pos_lb = lb > 0.0
        lb_safe = jnp.where(pos_lb, lb, 1.0)
        base = jnp.log1p(-lb) + jax.nn.log_sigmoid(hz)
        h_logf = jnp.where(pos_lb, jnp.logaddexp(jnp.log(lb_safe), base), base)
        h_k = (1.0 - lb) * jax.nn.sigmoid(-hz)
        return (cv_a, cv_g, _heads(fq), _heads(fk), _heads(fv), f_logf,
                _heads(hq), h_k, _heads(hi), h_logf, hgate, _heads(sq), _heads(sk), _heads(sv))

    def finish(x, l, conv_o, fox_o, hg_o, hgate, sb_o, mk, mv):
        hg = _rmsnorm(hg_o, hgrn_norm[l]) * jax.nn.silu(_heads(hgate).astype(f32))
        mix = jnp.concatenate([conv_o.astype(x.dtype), _flat(fox_o).astype(x.dtype),
                               _flat(hg).astype(x.dtype), _flat(sb_o).astype(x.dtype)], axis=-1)
        x = x + mix @ w_out[l]
        x = x + _mem_attend(_rmsnorm(x, norm_mem[l]), mk, mv, w_mq[l], w_mo[l])
        h = _rmsnorm(x, norm_ffn[l])
        return x + jnp.square(jax.nn.relu(h @ w_up[l])) @ w_down[l]

    bp, sp = x_prompt.shape[0], x_prompt.shape[1]
    pos_p = jnp.arange(sp)
    qpos_blocks = pos_p.reshape(-1, QBLK)
    x = x_prompt
    conv_pl, fk_pl, fv_pl, flf_pl, hg_pl, sk_pl, sv_pl, mk_pl, mv_pl = [], [], [], [], [], [], [], [], []
    for l in range(DEPTH):
        cv_a, cv_g, fq, fk, fv, flf, hq, hk, hv, hlf, hgate, sq, sk, sv = project(x, l)
        conv_o, conv_h = _conv_module(cv_a, cv_g, jnp.zeros((bp, CONV_W - 1, G_WIDTH), cv_a.dtype),
                                      conv_w[l], conv_b[l], conv_ln_g[l], conv_ln_b[l])
        fc = jnp.cumsum(flf, axis=1)
        fox_o = _unblocks(lax.map(lambda a: _fox_attend(a[0], a[1], a[2], fk, fv, fc, pos_p),
                                  (_blocks(fq, QBLK), _blocks(fc, QBLK), qpos_blocks)))
        s_fin, hg_o = lax.scan(lambda st, c: _hgrn_chunk(st, c[0], c[1], c[2], c[3]),
                               jnp.zeros((bp, N_HEADS, HEAD_DIM, HEAD_DIM), f32),
                               (_blocks(hq, CHUNK), _blocks(hk, CHUNK), _blocks(hv, CHUNK), _blocks(hlf, CHUNK)))
        hg_o = _unblocks(hg_o)
        sb_o = _unblocks(lax.map(lambda a: _sb_attend(a[0], a[1], sk, sv, pos_p),
                                 (_blocks(sq, QBLK), qpos_blocks)))
        mk = (mem_prompt @ w_mk[l]).reshape(bp, N_MEM, MEM_HEADS, MEM_HEAD_DIM)
        mv = (mem_prompt @ w_mv[l]).reshape(bp, N_MEM, MEM_HEADS, MEM_HEAD_DIM)
        x = finish(x, l, conv_o, fox_o, hg_o, hgate, sb_o, mk, mv)
        conv_pl.append(conv_h); fk_pl.append(fk); fv_pl.append(fv); flf_pl.append(flf)
        hg_pl.append(s_fin); sk_pl.append(sk); sv_pl.append(sv); mk_pl.append(mk); mv_pl.append(mv)
    y_prompt = _rmsnorm(x, norm_final)

    past = cache_fox_k.shape[2]
    ls = x_sample.shape[1]
    qpos = past + jnp.arange(ls)
    kpos = jnp.arange(past + ls)
    x = x_sample
    conv_sl, fk_sl, fv_sl, flf_sl, hg_sl, sk_sl, sv_sl = [], [], [], [], [], [], []
    for l in range(DEPTH):
        cv_a, cv_g, fq, fk, fv, flf, hq, hk, hv, hlf, hgate, sq, sk, sv = project(x, l)
        conv_o, conv_h = _conv_module(cv_a, cv_g, cache_conv[l], conv_w[l], conv_b[l], conv_ln_g[l], conv_ln_b[l])
        fk_all = jnp.concatenate([cache_fox_k[l].astype(fk.dtype), fk], axis=1)
        fv_all = jnp.concatenate([cache_fox_v[l].astype(fv.dtype), fv], axis=1)
        fc = jnp.cumsum(jnp.concatenate([cache_fox_logf[l].astype(f32), flf], axis=1), axis=1)
        fox_o = _fox_attend(fq, fc[:, past:], qpos, fk_all, fv_all, fc, kpos)
        s_new, hg_o = _hgrn_chunk(state_hgrn[l].astype(f32), hq, hk, hv, hlf)
        sk_all = jnp.concatenate([cache_sb_k[l].astype(sk.dtype), sk], axis=1)
        sv_all = jnp.concatenate([cache_sb_v[l].astype(sv.dtype), sv], axis=1)
        sb_o = _sb_attend(sq, qpos, sk_all, sv_all, kpos)
        x = finish(x, l, conv_o, fox_o, hg_o, hgate, sb_o, cache_mem_k[l], cache_mem_v[l])
        conv_sl.append(conv_h); fk_sl.append(fk); fv_sl.append(fv); flf_sl.append(flf)
        hg_sl.append(s_new); sk_sl.append(sk); sv_sl.append(sv)
    y_sample = _rmsnorm(x, norm_final)

    return (y_prompt, y_sample,
            jnp.stack(conv_pl), jnp.stack(fk_pl), jnp.stack(fv_pl), jnp.stack(flf_pl), jnp.stack(hg_pl),
            jnp.stack(sk_pl), jnp.stack(sv_pl), jnp.stack(mk_pl), jnp.stack(mv_pl),
            jnp.stack(conv_sl), jnp.stack(fk_sl), jnp.stack(fv_sl), jnp.stack(flf_sl), jnp.stack(hg_sl),
            jnp.stack(sk_sl), jnp.stack(sv_sl))
```

```python
import functools

import jax
import jax.numpy as jnp
from jax import lax
from jax.experimental import pallas as pl
from jax.experimental.pallas import tpu as pltpu

F32 = jnp.float32
BF16 = jnp.bfloat16

D_MODEL = 1024
G = 256
H = 4
HD = 64
CONV_W = 31
HALO = CONV_W - 1
N_MEM = 256
MEM_HEADS = 4
MEM_HD = 256
D_FF = 4096
EPS = 1e-6
NEG_BIG = -1e30
SUB = 16
CHUNK = 64
FF_PAD = 128
VMEM_LIMIT_BYTES = 52 * 1024 * 1024


def _params(*sem):
    return pltpu.CompilerParams(dimension_semantics=sem, vmem_limit_bytes=VMEM_LIMIT_BYTES)


def _logsig(x):
    return jnp.minimum(x, 0.0) - jnp.log1p(jnp.exp(-jnp.abs(x)))


def _sigmoid(x):
    return 1.0 / (1.0 + jnp.exp(-x))


def _rms(x, g):
    return x * lax.rsqrt(jnp.mean(x * x, axis=-1, keepdims=True) + EPS) * g


def _dot(a, b):
    return jnp.dot(a, b, preferred_element_type=F32)


def _dot_nt(a, b):
    return lax.dot_general(a, b, (((1,), (1,)), ((), ())), preferred_element_type=F32)


def _dot_tn(a, b):
    return lax.dot_general(a, b, (((0,), (0,)), ((), ())), preferred_element_type=F32)


def _stack_heads(x, lane_head):
    return jnp.concatenate([jnp.where(lane_head == h, x, jnp.zeros_like(x)) for h in range(H)], axis=0)


def _unstack_heads(x4, lane_head, t):
    out = jnp.zeros((t, G), x4.dtype)
    for h in range(H):
        out = jnp.where(lane_head == h, x4[h * t:(h + 1) * t], out)
    return out


def _proj_kernel(x_ref, g_ref, w_ref, bff_ref, hc_ref,
                 u_ref, fq_ref, fk_ref, fkb_ref, fv_ref, fvb_ref, flf_ref,
                 hq_ref, hk_ref, hv_ref, hlf_ref, hgate_ref,
                 sq_ref, sk_ref, skb_ref, sv_ref, svb_ref):
    h = _rms(x_ref[...], g_ref[...]).astype(BF16)

    def col(i, n=G):
        return _dot(h, w_ref[:, i * G:i * G + n])

    u_ref[...] = col(0) * _sigmoid(col(1))
    fq_ref[...] = (col(2) * (HD ** -0.5)).astype(BF16)
    fk = col(3)
    fk_ref[...] = fk
    fkb_ref[...] = fk.astype(BF16)
    fv = col(4)
    fv_ref[...] = fv
    fvb_ref[...] = fv.astype(BF16)
    hq_ref[...] = col(5)
    hz = col(6)
    base = hc_ref[0:1, :] + _logsig(hz)
    c1 = hc_ref[1:2, :]
    lae = jnp.maximum(c1, base) + jnp.log1p(jnp.exp(-jnp.abs(c1 - base)))
    hlf_ref[...] = jnp.where(hc_ref[2:3, :] > 0.5, lae, base)
    hk_ref[...] = hc_ref[3:4, :] * _sigmoid(-hz)
    hv_ref[...] = col(7)
    hgate_ref[...] = col(8)
    sq_ref[...] = (col(9) * (HD ** -0.5)).astype(BF16)
    sk = col(10)
    sk_ref[...] = sk
    skb_ref[...] = sk.astype(BF16)
    sv = col(11)
    sv_ref[...] = sv
    svb_ref[...] = sv.astype(BF16)
    flf_ref[...] = _logsig(col(12, FF_PAD) + bff_ref[...])


def _proj(x, g, w, bff, hc, tt):
    t = x.shape[0]
    ncol = w.shape[1]
    row = lambda i: (i, 0)
    fix = lambda i: (0, 0)
    f32o = jax.ShapeDtypeStruct((t, G), F32)
    b16o = jax.ShapeDtypeStruct((t, G), BF16)
    outs = [f32o, b16o, f32o, b16o, f32o, b16o, jax.ShapeDtypeStruct((t, FF_PAD), F32),
            f32o, f32o, f32o, f32o, f32o, b16o, f32o, b16o, f32o, b16o]
    ospecs = [pl.BlockSpec((tt, o.shape[1]), row) for o in outs]
    return pl.pallas_call(
        _proj_kernel,
        grid=(t // tt,),
        in_specs=[pl.BlockSpec((tt, D_MODEL), row), pl.BlockSpec((1, D_MODEL), fix),
                  pl.BlockSpec((D_MODEL, ncol), fix), pl.BlockSpec((1, FF_PAD), fix),
                  pl.BlockSpec((4, G), fix)],
        out_specs=ospecs,
        out_shape=outs,
        compiler_params=_params("parallel"),
        name="proj",
    )(x, g, w, bff, hc)


def _cumsum_kernel(x_ref, o_ref):
    x = x_ref[...]
    n = x.shape[1]
    lane = lax.broadcasted_iota(jnp.int32, x.shape, 1)
    sh = 1
    while sh < n:
        x = x + jnp.where(lane >= sh, pltpu.roll(x, sh, axis=1), 0.0)
        sh *= 2
    o_ref[...] = x


def _cumsum_lanes(x):
    return pl.pallas_call(
        _cumsum_kernel,
        out_shape=jax.ShapeDtypeStruct(x.shape, F32),
        compiler_params=pltpu.CompilerParams(vmem_limit_bytes=VMEM_LIMIT_BYTES),
        name="cumsum",
    )(x)


def _conv_kernel(u_ref, hist_ref, w_ref, p_ref, o_ref, ext_ref, *, tt):
    i = pl.program_id(1)
    base = 32 - HALO

    @pl.when(i == 0)
    def _():
        ext_ref[base:32, :] = hist_ref[0]

    ext_ref[32:32 + tt, :] = u_ref[...]
    acc = jnp.zeros((tt, G), F32)
    for j in range(CONV_W):
        acc = acc + w_ref[j:j + 1, :] * ext_ref[base + j:base + j + tt, :]
    yf = acc + p_ref[0:1, :]
    mu = jnp.mean(yf, axis=-1, keepdims=True)
    d = yf - mu
    var = jnp.mean(d * d, axis=-1, keepdims=True)
    yn = d * lax.rsqrt(var + EPS) * p_ref[1:2, :] + p_ref[2:3, :]
    o_ref[...] = (yn * _sigmoid(yn)).astype(BF16)
    ext_ref[base:32, :] = ext_ref[tt + base:tt + 32, :]


def _conv(u, hist, w, p, *, row_off, b, l, tt):
    nt = l // tt
    off = row_off // tt
    return pl.pallas_call(
        functools.partial(_conv_kernel, tt=tt),
        grid=(b, nt),
        in_specs=[pl.BlockSpec((tt, G), lambda bi, i: (off + bi * nt + i, 0)),
                  pl.BlockSpec((1, HALO, G), lambda bi, i: (bi, 0, 0)),
                  pl.BlockSpec((CONV_W, G), lambda bi, i: (0, 0)),
                  pl.BlockSpec((3, G), lambda bi, i: (0, 0))],
        out_specs=pl.BlockSpec((tt, G), lambda bi, i: (bi * nt + i, 0)),
        out_shape=jax.ShapeDtypeStruct((b * l, G), BF16),
        scratch_shapes=[pltpu.VMEM((tt + 32, G), F32)],
        compiler_params=_params("arbitrary", "arbitrary"),
        name="conv",
    )(u, hist, w, p)


def _fox_kernel(*refs, tq, tkp, n_past):
    if n_past:
        (q_ref, kc_ref, vc_ref, ckc_ref, cq_ref, kp_ref, vp_ref, ckp_ref,
         o_ref, m_ref, l_ref, acc_ref) = refs
    else:
        q_ref, kc_ref, vc_ref, ckc_ref, cq_ref, o_ref, m_ref, l_ref, acc_ref = refs
    i = pl.program_id(1)
    lane_head = lax.broadcasted_iota(jnp.int32, (tq, G), 1) >> 6
    q4 = _stack_heads(q_ref[...], lane_head)
    cq = cq_ref[...]
    tri = (lax.broadcasted_iota(jnp.int32, (tq, tq), 1)
           <= lax.broadcasted_iota(jnp.int32, (tq, tq), 0))
    m_ref[...] = jnp.full(m_ref.shape, NEG_BIG, F32)
    l_ref[...] = jnp.zeros(l_ref.shape, F32)
    acc_ref[...] = jnp.zeros(acc_ref.shape, F32)

    def step(k, v, ck, masked):
        s = _dot_nt(q4, k)
        ps = []
        for h in range(H):
            sl = slice(h * tq, (h + 1) * tq)
            sh = s[sl] - ck[h:h + 1, :]
            if masked:
                sh = jnp.where(tri, sh, NEG_BIG)
            cqh = cq[:, h:h + 1]
            m_old = m_ref[sl]
            m_new = jnp.maximum(m_old, jnp.max(sh, axis=-1, keepdims=True) + cqh)
            p = jnp.exp(sh - (m_new - cqh))
            alpha = jnp.exp(m_old - m_new)
            l_ref[sl] = alpha * l_ref[sl] + jnp.sum(p, axis=-1, keepdims=True)
            m_ref[sl] = m_new
            acc_ref[sl] = alpha * acc_ref[sl]
            ps.append(p.astype(BF16))
        acc_ref[...] += _dot(jnp.concatenate(ps, axis=0), v)

    if n_past:
        def past_body(j, c):
            st = pl.multiple_of(j * tkp, tkp)
            step(kp_ref[0, pl.ds(st, tkp), :].astype(BF16),
                 vp_ref[0, pl.ds(st, tkp), :].astype(BF16), ckp_ref[0, j], False)
            return c
        lax.fori_loop(0, n_past, past_body, 0)

    def cur_body(j, c):
        st = pl.multiple_of(j * tq, tq)
        step(kc_ref[pl.ds(st, tq), :], vc_ref[pl.ds(st, tq), :], ckc_ref[0, j], False)
        return c
    lax.fori_loop(0, i, cur_body, 0)

    st = pl.multiple_of(i * tq, tq)
    step(kc_ref[pl.ds(st, tq), :], vc_ref[pl.ds(st, tq), :], ckc_ref[0, i], True)

    acc4 = jnp.concatenate(
        [acc_ref[h * tq:(h + 1) * tq] * (1.0 / l_ref[h * tq:(h + 1) * tq]) for h in range(H)], axis=0)
    o_ref[...] = _unstack_heads(acc4, lane_head, tq).astype(BF16)


def _fox(q, kc, vc, ckc, cq, past, *, row_off, b, l, tq, tkp):
    nq = l // tq
    qoff = row_off // tq
    koff = row_off // l
    n_past = 0
    in_specs = [pl.BlockSpec((tq, G), lambda bi, i: (qoff + bi * nq + i, 0)),
                pl.BlockSpec((l, G), lambda bi, i: (koff + bi, 0)),
                pl.BlockSpec((l, G), lambda bi, i: (koff + bi, 0)),
                pl.BlockSpec((1, nq, H, tq), lambda bi, i: (bi, 0, 0, 0)),
                pl.BlockSpec((tq, H), lambda bi, i: (bi * nq + i, 0))]
    args = [q, kc, vc, ckc, cq]
    if past is not None:
        kp, vp, ckp = past
        n_past = kp.shape[1] // tkp
        in_specs += [pl.BlockSpec((1, kp.shape[1], G), lambda bi, i: (bi, 0, 0)),
                     pl.BlockSpec((1, kp.shape[1], G), lambda bi, i: (bi, 0, 0)),
                     pl.BlockSpec((1, n_past, H, tkp), lambda bi, i: (bi, 0, 0, 0))]
        args += [kp, vp, ckp]
    return pl.pallas_call(
        functools.partial(_fox_kernel, tq=tq, tkp=tkp, n_past=n_past),
        grid=(b, nq),
        in_specs=in_specs,
        out_specs=pl.BlockSpec((tq, G), lambda bi, i: (bi * nq + i, 0)),
        out_shape=jax.ShapeDtypeStruct((b * l, G), BF16),
        scratch_shapes=[pltpu.VMEM((H * tq, 1), F32), pltpu.VMEM((H * tq, 1), F32),
                        pltpu.VMEM((H * tq, G), F32)],
        compiler_params=_params("parallel", "arbitrary"),
        name="fox",
    )(*args)


def _sb_kernel(*refs, tq, tkp, n_past):
    if n_past:
        q_ref, kc_ref, vc_ref, kp_ref, vp_ref, o_ref, r_ref, acc_ref = refs
    else:
        q_ref, kc_ref, vc_ref, o_ref, r_ref, acc_ref = refs
    i = pl.program_id(1)
    lane_head = lax.broadcasted_iota(jnp.int32, (tq, G), 1) >> 6
    q4 = _stack_heads(q_ref[...], lane_head)
    r_ref[...] = jnp.zeros(r_ref.shape, F32)
    acc_ref[...] = jnp.zeros(acc_ref.shape, F32)

    def step(k, v, masked):
        tk = k.shape[0]
        z = _dot_nt(q4, k)
        sp = jnp.maximum(z, 0.0) + jnp.log1p(jnp.exp(-jnp.abs(z)))
        u = -sp
        if masked:
            tri = (lax.broadcasted_iota(jnp.int32, (tq, tk), 1)
                   < lax.broadcasted_iota(jnp.int32, (tq, tk), 0))
            valid = jnp.concatenate([tri] * H, axis=0)
            u = jnp.where(valid, u, 0.0)
        later = (lax.broadcasted_iota(jnp.int32, (tk, tk), 0)
                 > lax.broadcasted_iota(jnp.int32, (tk, tk), 1))
        later = jnp.where(later, 1.0, 0.0).astype(BF16)
        rest = _dot(u.astype(BF16), later) + r_ref[...]
        a = jnp.exp(z - sp + rest)
        if masked:
            a = jnp.where(valid, a, 0.0)
        acc_ref[...] += _dot(a.astype(BF16), v)
        r_ref[...] += jnp.sum(u, axis=-1, keepdims=True)

    st = pl.multiple_of(i * tq, tq)
    step(kc_ref[pl.ds(st, tq), :], vc_ref[pl.ds(st, tq), :], True)

    def cur_body(n, c):
        j = i - 1 - n
        s0 = pl.multiple_of(j * tq, tq)
        step(kc_ref[pl.ds(s0, tq), :], vc_ref[pl.ds(s0, tq), :], False)
        return c
    lax.fori_loop(0, i, cur_body, 0)

    if n_past:
        def past_body(n, c):
            j = n_past - 1 - n
            s0 = pl.multiple_of(j * tkp, tkp)
            step(kp_ref[0, pl.ds(s0, tkp), :].astype(BF16),
                 vp_ref[0, pl.ds(s0, tkp), :].astype(BF16), False)
            return c
        lax.fori_loop(0, n_past, past_body, 0)

    o_ref[...] = _unstack_heads(acc_ref[...], lane_head, tq).astype(BF16)


def _sb(q, kc, vc, past, *, row_off, b, l, tq, tkp):
    nq = l // tq
    qoff = row_off // tq
    koff = row_off // l
    n_past = 0
    in_specs = [pl.BlockSpec((tq, G), lambda bi, i: (qoff + bi * nq + i, 0)),
                pl.BlockSpec((l, G), lambda bi, i: (koff + bi, 0)),
                pl.BlockSpec((l, G), lambda bi, i: (koff + bi, 0))]
    args = [q, kc, vc]
    if past is not None:
        kp, vp = past
        n_past = kp.shape[1] // tkp
        in_specs += [pl.BlockSpec((1, kp.shape[1], G), lambda bi, i: (bi, 0, 0)),
                     pl.BlockSpec((1, kp.shape[1], G), lambda bi, i: (bi, 0, 0))]
        args += [kp, vp]
    return pl.pallas_call(
        functools.partial(_sb_kernel, tq=tq, tkp=tkp, n_past=n_past),
        grid=(b, nq),
        in_specs=in_specs,
        out_specs=pl.BlockSpec((tq, G), lambda bi, i: (bi * nq + i, 0)),
        out_shape=jax.ShapeDtypeStruct((b * l, G), BF16),
        scratch_shapes=[pltpu.VMEM((H * tq, 1), F32), pltpu.VMEM((H * tq, G), F32)],
        compiler_params=_params("parallel", "arbitrary"),
        name="sb",
    )(*args)


def _hgrn_kernel(*refs, tt, has_init):
    if has_init:
        (q_ref, k_ref, v_ref, lf_ref, gate_ref, hn_ref, st0_ref,
         o_ref, sto_ref, st_ref, kpad, vpad, bpad) = refs
    else:
        (q_ref, k_ref, v_ref, lf_ref, gate_ref, hn_ref,
         o_ref, sto_ref, st_ref, kpad, vpad, bpad) = refs
    i = pl.program_id(1)
    nc = tt // CHUNK

    @pl.when(i == 0)
    def _():
        if has_init:
            st_ref[...] = st0_ref[0]
        else:
            st_ref[...] = jnp.zeros(st_ref.shape, F32)

    q = q_ref[...]
    kk = k_ref[...]
    v = v_ref[...]
    row = lax.broadcasted_iota(jnp.int32, (tt, G), 0)
    r64 = row & (CHUNK - 1)
    r16 = row & (SUB - 1)
    lane_head = lax.broadcasted_iota(jnp.int32, (tt, G), 1) >> 6
    same_head = (lax.broadcasted_iota(jnp.int32, (G, G), 0) >> 6
                 == lax.broadcasted_iota(jnp.int32, (G, G), 1) >> 6)
    bd = jnp.where(same_head, 1.0, 0.0).astype(BF16)

    b = lf_ref[...]
    sh = 1
    while sh < CHUNK:
        b = b + jnp.where(r64 >= sh, pltpu.roll(b, sh, axis=0), 0.0)
        sh *= 2

    def chunk_row(r):
        return jnp.concatenate(
            [jnp.broadcast_to(b[c * CHUNK + r:c * CHUNK + r + 1, :], (CHUNK, G)) for c in range(nc)], axis=0)

    zpad = jnp.zeros((SUB, G), F32)
    kpad[0:SUB, :] = zpad
    vpad[0:SUB, :] = zpad
    bpad[0:SUB, :] = zpad
    kpad[SUB:SUB + tt, :] = kk
    vpad[SUB:SUB + tt, :] = v
    bpad[SUB:SUB + tt, :] = b
    o = jnp.zeros((tt, G), F32)
    for d in range(SUB):
        ks = kpad[SUB - d:SUB - d + tt, :]
        vs = vpad[SUB - d:SUB - d + tt, :]
        bs = bpad[SUB - d:SUB - d + tt, :]
        p = q * ks * jnp.exp(jnp.minimum(b - bs, 0.0))
        p = jnp.where(r16 >= d, p, 0.0)
        o = o + _dot(p.astype(BF16), bd) * vs

    refs_b = [chunk_row(SUB * n - 1) for n in range(1, CHUNK // SUB)]
    sub = r64 >> 4
    rq = b
    for n, rb in enumerate(refs_b, start=1):
        rq = jnp.where(sub == n, rb, rq)
    q4 = _stack_heads((q * jnp.exp(jnp.minimum(b - rq, 0.0))).astype(BF16), lane_head)
    tr = lax.broadcasted_iota(jnp.int32, (tt, tt), 0)
    sc = lax.broadcasted_iota(jnp.int32, (tt, tt), 1)
    tsub = (tr & (CHUNK - 1)) >> 4
    earlier = jnp.where(tr >> 6 == sc >> 6, (sc & (CHUNK - 1)) >> 4, CHUNK) < tsub
    sel = jnp.where(earlier, tsub, 0)
    scores = []
    for rb in refs_b:
        kt = (kk * jnp.exp(jnp.minimum(rb - b, 0.0))).astype(BF16)
        scores.append(_dot_nt(q4, kt))
    a4 = []
    for h in range(H):
        sl = slice(h * tt, (h + 1) * tt)
        a = jnp.zeros((tt, tt), F32)
        for n, s in enumerate(scores, start=1):
            a = jnp.where(sel == n, s[sl], a)
        a4.append(a.astype(BF16))
    vb = v.astype(BF16)
    o = o + _unstack_heads(_dot(jnp.concatenate(a4, axis=0), vb), lane_head, tt)

    blast = chunk_row(CHUNK - 1)
    qs = (q * jnp.exp(b)).astype(BF16)
    kd = (kk * jnp.exp(jnp.minimum(blast - b, 0.0))).astype(BF16)
    dec = jnp.exp(blast)
    outs = []
    for c in range(nc):
        sl = slice(c * CHUNK, (c + 1) * CHUNK)
        st = st_ref[...]
        outs.append(_dot_nt(qs[sl], st.astype(BF16)))
        upd = _dot_tn(vb[sl], kd[sl])
        st_ref[...] = st * dec[c * CHUNK:c * CHUNK + 1, :] + jnp.where(same_head, upd, 0.0)
    o = o + jnp.concatenate(outs, axis=0)

    sq = o * o
    hi = sq.astype(BF16)
    lo = (sq - hi.astype(F32)).astype(BF16)
    ms = (_dot(hi, bd) + _dot(lo, bd)) * (1.0 / HD)
    gate = gate_ref[...]
    o_ref[...] = (o * lax.rsqrt(ms + EPS) * hn_ref[...] * (gate * _sigmoid(gate))).astype(BF16)

    @pl.when(i == pl.num_programs(1) - 1)
    def _():
        sto_ref[0] = st_ref[...]


def _hgrn(q, k, v, lf, gate, hn, st0, *, row_off, b, l, tt):
    nt = l // tt
    off = row_off // tt
    tok = pl.BlockSpec((tt, G), lambda bi, i: (off + bi * nt + i, 0))
    in_specs = [tok, tok, tok, tok, tok, pl.BlockSpec((1, G), lambda bi, i: (0, 0))]
    args = [q, k, v, lf, gate, hn]
    if st0 is not None:
        in_specs.append(pl.BlockSpec((1, G, G), lambda bi, i: (bi, 0, 0)))
        args.append(st0)
    return pl.pallas_call(
        functools.partial(_hgrn_kernel, tt=tt, has_init=st0 is not None),
        grid=(b, nt),
        in_specs=in_specs,
        out_specs=[pl.BlockSpec((tt, G), lambda bi, i: (bi * nt + i, 0)),
                   pl.BlockSpec((1, G, G), lambda bi, i: (bi, 0, 0))],
        out_shape=[jax.ShapeDtypeStruct((b * l, G), BF16), jax.ShapeDtypeStruct((b, G, G), F32)],
        scratch_shapes=[pltpu.VMEM((G, G), F32)] + [pltpu.VMEM((tt + SUB, G), F32)] * 3,
        compiler_params=_params("arbitrary", "arbitrary"),
        name="hgrn",
    )(*args)


def _c1_kernel(x_ref, m0_ref, m1_ref, m2_ref, m3_ref, wo_ref, g_ref, wq_ref, x1_ref, q_ref):
    acc = x_ref[...]
    for p, m_ref in enumerate((m0_ref, m1_ref, m2_ref, m3_ref)):
        acc = acc + _dot(m_ref[...], wo_ref[p * G:(p + 1) * G, :])
    x1_ref[...] = acc
    h = _rms(acc, g_ref[...]).astype(BF16)
    q_ref[...] = (_dot(h, wq_ref[...]) * (MEM_HD ** -0.5)).astype(BF16)


def _c1(x, mix, wo, g, wq, tt):
    t = x.shape[0]
    row = lambda i: (i, 0)
    fix = lambda i: (0, 0)
    return pl.pallas_call(
        _c1_kernel,
        grid=(t // tt,),
        in_specs=[pl.BlockSpec((tt, D_MODEL), row)] + [pl.BlockSpec((tt, G), row)] * 4
                 + [pl.BlockSpec((D_MODEL, D_MODEL), fix), pl.BlockSpec((1, D_MODEL), fix),
                    pl.BlockSpec((D_MODEL, D_MODEL), fix)],
        out_specs=[pl.BlockSpec((tt, D_MODEL), row), pl.BlockSpec((tt, D_MODEL), row)],
        out_shape=[jax.ShapeDtypeStruct((t, D_MODEL), F32), jax.ShapeDtypeStruct((t, D_MODEL), BF16)],
        compiler_params=_params("parallel"),
        name="outproj_memq",
    )(x, *mix, wo, g, wq)


def _memkv_kernel(m_ref, wk_ref, wv_ref, k_ref, kb_ref, v_ref, vb_ref):
    m = m_ref[...].astype(BF16)
    k = _dot(m, wk_ref[...])
    k_ref[...] = k
    kb_ref[...] = k.astype(BF16)
    v = _dot(m, wv_ref[...])
    v_ref[...] = v
    vb_ref[...] = v.astype(BF16)


def _memkv(mem, wk, wv, tt):
    t = mem.shape[0]
    row = lambda i: (i, 0)
    fix = lambda i: (0, 0)
    f32o = jax.ShapeDtypeStruct((t, D_MODEL), F32)
    b16o = jax.ShapeDtypeStruct((t, D_MODEL), BF16)
    return pl.pallas_call(
        _memkv_kernel,
        grid=(t // tt,),
        in_specs=[pl.BlockSpec((tt, D_MODEL), row), pl.BlockSpec((D_MODEL, D_MODEL), fix),
                  pl.BlockSpec((D_MODEL, D_MODEL), fix)],
        out_specs=[pl.BlockSpec((tt, D_MODEL), row)] * 4,
        out_shape=[f32o, b16o, f32o, b16o],
        compiler_params=_params("parallel"),
        name="memkv",
    )(mem, wk, wv)


def _memattn_kernel(q_ref, k_ref, v_ref, o_ref):
    outs = []
    for h in range(MEM_HEADS):
        sl = slice(h * MEM_HD, (h + 1) * MEM_HD)
        s = _dot_nt(q_ref[:, sl], k_ref[:, sl].astype(BF16))
        p = jnp.exp(s - jnp.max(s, axis=-1, keepdims=True))
        den = jnp.sum(p, axis=-1, keepdims=True)
        outs.append(_dot(p.astype(BF16), v_ref[:, sl].astype(BF16)) * (1.0 / den))
    o_ref[...] = jnp.concatenate(outs, axis=-1).astype(BF16)


def _memattn(q, k, v, *, row_off, b, l, tq):
    nq = l // tq
    off = row_off // tq
    return pl.pallas_call(
        _memattn_kernel,
        grid=(b, nq),
        in_specs=[pl.BlockSpec((tq, D_MODEL), lambda bi, i: (off + bi * nq + i, 0)),
                  pl.BlockSpec((N_MEM, D_MODEL), lambda bi, i: (bi, 0)),
                  pl.BlockSpec((N_MEM, D_MODEL), lambda bi, i: (bi, 0))],
        out_specs=pl.BlockSpec((tq, D_MODEL), lambda bi, i: (bi * nq + i, 0)),
        out_shape=jax.ShapeDtypeStruct((b * l, D_MODEL), BF16),
        compiler_params=_params("parallel", "parallel"),
        name="memattn",
    )(q, k, v)


def _c3_kernel(x1_ref, o_ref, wmo_ref, g_ref, wup_ref, wdn_ref, gf_ref, y_ref,
               x2_ref, h_ref, acc_ref, *, final):
    f = pl.program_id(1)

    @pl.when(f == 0)
    def _():
        x2 = x1_ref[...] + _dot(o_ref[...], wmo_ref[...])
        x2_ref[...] = x2
        h_ref[...] = _rms(x2, g_ref[...]).astype(BF16)
        acc_ref[...] = jnp.zeros(acc_ref.shape, F32)

    a = jnp.maximum(_dot(h_ref[...], wup_ref[...]), 0.0)
    acc_ref[...] += _dot((a * a).astype(BF16), wdn_ref[...])

    @pl.when(f == pl.num_programs(1) - 1)
    def _():
        x3 = x2_ref[...] + acc_ref[...]
        y_ref[...] = _rms(x3, gf_ref[...]) if final else x3


def _c3(x1, o, wmo, g, wup, wdn, gf, *, tt, tf, final):
    t = x1.shape[0]
    row = lambda i, f: (i, 0)
    fix = lambda i, f: (0, 0)
    return pl.pallas_call(
        functools.partial(_c3_kernel, final=final),
        grid=(t // tt, D_FF // tf),
        in_specs=[pl.BlockSpec((tt, D_MODEL), row), pl.BlockSpec((tt, D_MODEL), row),
                  pl.BlockSpec((D_MODEL, D_MODEL), fix), pl.BlockSpec((1, D_MODEL), fix),
                  pl.BlockSpec((D_MODEL, tf), lambda i, f: (0, f)),
                  pl.BlockSpec((tf, D_MODEL), lambda i, f: (f, 0)),
                  pl.BlockSpec((1, D_MODEL), fix)],
        out_specs=pl.BlockSpec((tt, D_MODEL), row),
        out_shape=jax.ShapeDtypeStruct((t, D_MODEL), F32),
        scratch_shapes=[pltpu.VMEM((tt, D_MODEL), F32), pltpu.VMEM((tt, D_MODEL), BF16),
                        pltpu.VMEM((tt, D_MODEL), F32)],
        compiler_params=_params("parallel", "arbitrary"),
        name="memout_mlp",
    )(x1, o, wmo, g, wup, wdn, gf)


def _block_diag_t(s):
    b = s.shape[0]
    eye = jnp.eye(H, dtype=s.dtype)
    st = jnp.swapaxes(s, 2, 3)
    return jnp.einsum('bhvk,hg->bhvgk', st, eye).reshape(b, G, G)


def _unblock_diag_t(st):
    b = st.shape[0]
    s5 = st.reshape(b, H, HD, H, HD)
    d = jnp.stack([s5[:, h, :, h, :] for h in range(H)], axis=1)
    return jnp.swapaxes(d, 2, 3)


def _row_cumsum(lf, tile):
    b, s, _ = lf.shape
    sp = -(-s // tile) * tile
    x = jnp.swapaxes(lf, 1, 2).reshape(b * H, s)
    x = jnp.pad(x, ((0, 0), (0, sp - s)))
    return _cumsum_lanes(x).reshape(b, H, sp)[:, :, :s]


def _tiles(c, n, t):
    b = c.shape[0]
    return jnp.swapaxes(c.reshape(b, H, n, t), 1, 2)


def kernel(x_prompt, x_sample, mem_prompt, cache_conv, cache_fox_k, cache_fox_v, cache_fox_logf,
           state_hgrn, cache_sb_k, cache_sb_v, cache_mem_k, cache_mem_v, norm_mix, w_in, b_fox_f,
           conv_w, conv_b, conv_ln_g, conv_ln_b, hgrn_lb, hgrn_norm, w_out, norm_mem, w_mq, w_mk,
           w_mv, w_mo, norm_ffn, w_up, w_down, norm_final):
    bp, sp, _ = x_prompt.shape
    bs, ls, _ = x_sample.shape
    depth = w_in.shape[0]
    past = cache_fox_k.shape[2]
    tp = bp * sp
    ts = bs * ls
    tq_p = 256
    tk_past = 256

    sm = jax.nn.softmax(hgrn_lb.astype(F32), axis=0)
    lower = jnp.clip(jnp.cumsum(sm, axis=0) - sm[0], 0.0, 1.0 - 1e-6)
    pos = lower > 0.0
    hconst = jnp.stack([jnp.log1p(-lower), jnp.log(jnp.where(pos, lower, 1.0)),
                        pos.astype(F32), 1.0 - lower], axis=1)

    o_ff = 5 * G
    w_in_r = jnp.concatenate(
        [w_in[:, :, :o_ff], w_in[:, :, o_ff + H:],
         jnp.pad(w_in[:, :, o_ff:o_ff + H], ((0, 0), (0, 0), (0, FF_PAD - H)))], axis=-1).astype(BF16)
    bff = jnp.pad(b_fox_f, ((0, 0), (0, FF_PAD - H)))[:, None, :]
    w_out_b = w_out.astype(BF16)
    w_mq_b = w_mq.astype(BF16)
    w_mk_b = w_mk.astype(BF16)
    w_mv_b = w_mv.astype(BF16)
    w_mo_b = w_mo.astype(BF16)
    w_up_b = w_up.astype(BF16)
    w_down_b = w_down.astype(BF16)
    hn = jnp.tile(hgrn_norm, (1, H))[:, None, :]
    conv_p3 = jnp.stack([conv_b, conv_ln_g, conv_ln_b], axis=1)

    x = jnp.concatenate([x_prompt.reshape(tp, D_MODEL), x_sample.reshape(ts, D_MODEL)], axis=0)
    mem2 = mem_prompt.reshape(bp * N_MEM, D_MODEL)
    zero_hist = jnp.zeros((bp, HALO, G), F32)

    outs = {n: [] for n in ('conv_p', 'fk_p', 'fv_p', 'flf_p', 'hg_p', 'sk_p', 'sv_p', 'mk_p', 'mv_p',
                            'conv_s', 'fk_s', 'fv_s', 'flf_s', 'hg_s', 'sk_s', 'sv_s')}
    for l in range(depth):
        (u, fq, fk, fkb, fv, fvb, flf, hq, hk, hv, hlf, hgate,
         sq, sk, skb, sv, svb) = _proj(x, norm_mix[l][None], w_in_r[l], bff[l], hconst[l], 512)

        conv_op = _conv(u, zero_hist, conv_w[l], conv_p3[l], row_off=0, b=bp, l=sp, tt=512)
        conv_os = _conv(u, cache_conv[l], conv_w[l], conv_p3[l], row_off=tp, b=bs, l=ls, tt=ls)

        flf_p = flf[:tp, :H].reshape(bp, sp, H)
        flf_s = flf[tp:, :H].reshape(bs, ls, H)
        c_p = _row_cumsum(flf_p, 128)
        c_s = _row_cumsum(jnp.concatenate([cache_fox_logf[l].astype(F32), flf_s], axis=1), 128)
        fox_op = _fox(fq, fkb, fvb, _tiles(c_p, sp // tq_p, tq_p),
                      jnp.swapaxes(c_p, 1, 2).reshape(tp, H), None,
                      row_off=0, b=bp, l=sp, tq=tq_p, tkp=tk_past)
        fox_os = _fox(fq, fkb, fvb, _tiles(c_s[:, :, past:], 1, ls),
                      jnp.swapaxes(c_s[:, :, past:], 1, 2).reshape(ts, H),
                      (cache_fox_k[l].reshape(bs, past, G), cache_fox_v[l].reshape(bs, past, G),
                       _tiles(c_s[:, :, :past], past // tk_past, tk_past)),
                      row_off=tp, b=bs, l=ls, tq=ls, tkp=tk_past)

        hg_op, st_p = _hgrn(hq, hk, hv, hlf, hgate, hn[l], None, row_off=0, b=bp, l=sp, tt=tq_p)
        hg_os, st_s = _hgrn(hq, hk, hv, hlf, hgate, hn[l], _block_diag_t(state_hgrn[l].astype(F32)),
                            row_off=tp, b=bs, l=ls, tt=ls)

        sb_op = _sb(sq, skb, svb, None, row_off=0, b=bp, l=sp, tq=tq_p, tkp=tk_past)
        sb_os = _sb(sq, skb, svb,
                    (cache_sb_k[l].reshape(bs, past, G), cache_sb_v[l].reshape(bs, past, G)),
                    row_off=tp, b=bs, l=ls, tq=ls, tkp=tk_past)

        mix = [jnp.concatenate([a, b_], axis=0) for a, b_ in
               ((conv_op, conv_os), (fox_op, fox_os), (hg_op, hg_os), (sb_op, sb_os))]
        x1, qm = _c1(x, mix, w_out_b[l], norm_mem[l][None], w_mq_b[l], 512)

        mk, mkb, mv, mvb = _memkv(mem2, w_mk_b[l], w_mv_b[l], N_MEM)
        om_p = _memattn(qm, mkb, mvb, row_off=0, b=bp, l=sp, tq=512)
        om_s = _memattn(qm, cache_mem_k[l].reshape(bs * N_MEM, D_MODEL),
                        cache_mem_v[l].reshape(bs * N_MEM, D_MODEL), row_off=tp, b=bs, l=ls, tq=ls)
        om = jnp.concatenate([om_p, om_s], axis=0)

        x = _c3(x1, om, w_mo_b[l], norm_ffn[l][None], w_up_b[l], w_down_b[l], norm_final[None],
                tt=512, tf=1024, final=(l == depth - 1))

        u_p = u[:tp].reshape(bp, sp, G)
        u_s = jnp.concatenate([cache_conv[l].astype(F32), u[tp:].reshape(bs, ls, G)], axis=1)
        outs['conv_p'].append(u_p[:, sp - HALO:])
        outs['conv_s'].append(u_s[:, ls:])
        for name, arr in (('fk', fk), ('fv', fv), ('sk', sk), ('sv', sv)):
            outs[name + '_p'].append(arr[:tp].reshape(bp, sp, H, HD))
            outs[name + '_s'].append(arr[tp:].reshape(bs, ls, H, HD))
        outs['flf_p'].append(flf_p)
        outs['flf_s'].append(flf_s)
        outs['hg_p'].append(_unblock_diag_t(st_p))
        outs['hg_s'].append(_unblock_diag_t(st_s))
        outs['mk_p'].append(mk.reshape(bp, N_MEM, MEM_HEADS, MEM_HD))
        outs['mv_p'].append(mv.reshape(bp, N_MEM, MEM_HEADS, MEM_HD))

    y_prompt = x[:tp].reshape(bp, sp, D_MODEL)
    y_sample = x[tp:].reshape(bs, ls, D_MODEL)
    st = lambda n: jnp.stack(outs[n])
    return (y_prompt, y_sample,
            st('conv_p'), st('fk_p'), st('fv_p'), st('flf_p'), st('hg_p'),
            st('sk_p'), st('sv_p'), st('mk_p'), st('mv_p'),
            st('conv_s'), st('fk_s'), st('fv_s'), st('flf_s'), st('hg_s'),
            st('sk_s'), st('sv_s'))
```

```python
import functools

import jax
import jax.numpy as jnp
from jax import lax
from jax.experimental import pallas as pl
from jax.experimental.pallas import tpu as pltpu

F32 = jnp.float32
BF16 = jnp.bfloat16

D_MODEL = 1024
G = 256
H = 4
HD = 64
CONV_W = 31
HALO = CONV_W - 1
N_MEM = 256
MEM_HEADS = 4
MEM_HD = 256
D_FF = 4096
EPS = 1e-6
NEG_BIG = -1e30
SUB = 16
CHUNK = 64
FF_PAD = 128
LANES = 128
SB_BLOCK = 256
VMEM_LIMIT_BYTES = 52 * 1024 * 1024


def _params(*sem):
    return pltpu.CompilerParams(dimension_semantics=sem, vmem_limit_bytes=VMEM_LIMIT_BYTES)


def _logsig(x):
    return jnp.minimum(x, 0.0) - jnp.log(1.0 + jnp.exp(-jnp.abs(x)))


def _sigmoid(x):
    return 1.0 / (1.0 + jnp.exp(-x))


def _rms(x, g):
    return x * lax.rsqrt(jnp.mean(x * x, axis=-1, keepdims=True) + EPS) * g


def _dot(a, b):
    return jnp.dot(a, b, preferred_element_type=F32)


def _dot_nt(a, b):
    return lax.dot_general(a, b, (((1,), (1,)), ((), ())), preferred_element_type=F32)


def _dot_tn(a, b):
    return lax.dot_general(a, b, (((0,), (0,)), ((), ())), preferred_element_type=F32)


def _stack_heads(x, lane_head):
    return jnp.concatenate([jnp.where(lane_head == h, x, jnp.zeros_like(x)) for h in range(H)], axis=0)


def _unstack_heads(x4, lane_head, t):
    out = jnp.zeros((t, G), x4.dtype)
    for h in range(H):
        out = jnp.where(lane_head == h, x4[h * t:(h + 1) * t], out)
    return out


def _proj_groups(x_ref, g_ref, w_ref, bff_ref, hc_ref):
    h = _rms(x_ref[...], g_ref[...]).astype(BF16)

    def col(i, n=G):
        return _dot(h, w_ref[:, i * G:i * G + n])

    yield 'u', col(0) * _sigmoid(col(1))
    yield 'fq', col(2) * (HD ** -0.5)
    yield 'fk', col(3)
    yield 'fv', col(4)
    yield 'hq', col(5)
    hz = col(6)
    base = hc_ref[0:1, :] + _logsig(hz)
    c1 = hc_ref[1:2, :]
    lae = jnp.maximum(c1, base) + jnp.log(1.0 + jnp.exp(-jnp.abs(c1 - base)))
    yield 'hlf', jnp.where(hc_ref[2:3, :] > 0.5, lae, base)
    yield 'hk', hc_ref[3:4, :] * _sigmoid(-hz)
    yield 'hv', col(7)
    yield 'hgate', col(8)
    yield 'sq', col(9) * (HD ** -0.5)
    yield 'sk', col(10)
    yield 'sv', col(11)
    yield 'flf', _logsig(col(12, FF_PAD) + bff_ref[...])


_PROJ_ROW_F32 = ('u', 'hq', 'hlf', 'hk', 'hv', 'hgate')
_PROJ_ROW_BF16 = ('fq', 'sq')
_PROJ_KV = ('fk', 'fv', 'sk', 'sv')


def _proj_rows_kernel(x_ref, g_ref, w_ref, bff_ref, hc_ref, *out_refs):
    names = _PROJ_ROW_F32 + _PROJ_ROW_BF16 + ('flf',) + tuple(n + s for n in _PROJ_KV for s in ('', '_b'))
    refs = dict(zip(names, out_refs))
    for name, val in _proj_groups(x_ref, g_ref, w_ref, bff_ref, hc_ref):
        if name in _PROJ_KV:
            refs[name][...] = val
            refs[name + '_b'][...] = val.astype(BF16)
        else:
            refs[name][...] = val.astype(refs[name].dtype)


def _proj_rows(x, g, w, bff, hc, tt):
    t = x.shape[0]
    row = lambda i: (i, 0)
    fix = lambda i: (0, 0)
    f32o = jax.ShapeDtypeStruct((t, G), F32)
    b16o = jax.ShapeDtypeStruct((t, G), BF16)
    outs = ([f32o] * len(_PROJ_ROW_F32) + [b16o] * len(_PROJ_ROW_BF16)
            + [jax.ShapeDtypeStruct((t, FF_PAD), F32)] + [f32o, b16o] * len(_PROJ_KV))
    names = _PROJ_ROW_F32 + _PROJ_ROW_BF16 + ('flf',) + tuple(n + s for n in _PROJ_KV for s in ('', '_b'))
    res = pl.pallas_call(
        _proj_rows_kernel,
        grid=(t // tt,),
        in_specs=[pl.BlockSpec((tt, D_MODEL), row), pl.BlockSpec((1, D_MODEL), fix),
                  pl.BlockSpec((D_MODEL, w.shape[1]), fix), pl.BlockSpec((1, FF_PAD), fix),
                  pl.BlockSpec((4, G), fix)],
        out_specs=[pl.BlockSpec((tt, o.shape[1]), row) for o in outs],
        out_shape=outs,
        compiler_params=_params("parallel"),
        name="proj_rows",
    )(x, g, w, bff, hc)
    return dict(zip(names, res))


def _proj_seq_kernel(x_ref, g_ref, w_ref, bff_ref, hc_ref, *rest, n_alias):
    out_refs = rest[n_alias:]
    names = (_PROJ_ROW_F32 + _PROJ_ROW_BF16 + ('flf',)
             + tuple(n + s for n in _PROJ_KV for s in ('_t', '_tb')))
    refs = dict(zip(names, out_refs))
    for name, val in _proj_groups(x_ref, g_ref, w_ref, bff_ref, hc_ref):
        if name in _PROJ_KV:
            vt = val.T
            refs[name + '_t'][0, 0] = vt
            refs[name + '_tb'][0, 0] = vt.astype(BF16)
        elif name == 'flf':
            refs[name][0] = val.T[0:8, :]
        else:
            refs[name][...] = val.astype(refs[name].dtype)


def _proj_seq(x, g, w, bff, hc, bufs, *, layer, depth, b, l, tt):
    t = x.shape[0]
    nt = l // tt
    row = lambda i: (i, 0)
    fix = lambda i: (0, 0)
    f32o = jax.ShapeDtypeStruct((t, G), F32)
    b16o = jax.ShapeDtypeStruct((t, G), BF16)
    stacked = jax.ShapeDtypeStruct((depth, b, G, l), F32)
    tiled = jax.ShapeDtypeStruct((b, nt, G, tt), BF16)
    outs = ([f32o] * len(_PROJ_ROW_F32) + [b16o] * len(_PROJ_ROW_BF16)
            + [jax.ShapeDtypeStruct((b, 8, l), F32)] + [stacked, tiled] * len(_PROJ_KV))
    ospecs = ([pl.BlockSpec((tt, G), row)] * (len(_PROJ_ROW_F32) + len(_PROJ_ROW_BF16))
              + [pl.BlockSpec((1, 8, tt), lambda i: (i // nt, 0, i % nt))]
              + [pl.BlockSpec((1, 1, G, tt), lambda i: (layer, i // nt, 0, i % nt)),
                 pl.BlockSpec((1, 1, G, tt), lambda i: (i // nt, i % nt, 0, 0))] * len(_PROJ_KV))
    names = (_PROJ_ROW_F32 + _PROJ_ROW_BF16 + ('flf',)
             + tuple(n + s for n in _PROJ_KV for s in ('_t', '_tb')))
    n_in = 5
    first_stacked = len(_PROJ_ROW_F32) + len(_PROJ_ROW_BF16) + 1
    aliases = {}
    alias_args = []
    if bufs is not None:
        alias_args = list(bufs)
        aliases = {n_in + k: first_stacked + 2 * k for k in range(len(_PROJ_KV))}
    res = pl.pallas_call(
        functools.partial(_proj_seq_kernel, n_alias=len(alias_args)),
        grid=(t // tt,),
        in_specs=[pl.BlockSpec((tt, D_MODEL), row), pl.BlockSpec((1, D_MODEL), fix),
                  pl.BlockSpec((D_MODEL, w.shape[1]), fix), pl.BlockSpec((1, FF_PAD), fix),
                  pl.BlockSpec((4, G), fix)] + [pl.BlockSpec(memory_space=pl.ANY)] * len(alias_args),
        out_specs=ospecs,
        out_shape=outs,
        input_output_aliases=aliases,
        compiler_params=_params("parallel"),
        name="proj_seq",
    )(x, g, w, bff, hc, *alias_args)
    return dict(zip(names, res))


def _cumsum_kernel(x_ref, o_ref):
    x = x_ref[...]
    n = x.shape[1]
    lane = lax.broadcasted_iota(jnp.int32, x.shape, 1)
    sh = 1
    while sh < n:
        x = x + jnp.where(lane >= sh, pltpu.roll(x, sh, axis=1), 0.0)
        sh *= 2
    o_ref[...] = x


def _cumsum_lanes(x):
    return pl.pallas_call(
        _cumsum_kernel,
        out_shape=jax.ShapeDtypeStruct(x.shape, F32),
        compiler_params=pltpu.CompilerParams(vmem_limit_bytes=VMEM_LIMIT_BYTES),
        name="cumsum",
    )(x)


def _conv_kernel(u_ref, hist_ref, w_ref, p_ref, o_ref, ext_ref, *, tt):
    i = pl.program_id(1)
    base = 32 - HALO

    @pl.when(i == 0)
    def _():
        ext_ref[base:32, :] = hist_ref[0]

    ext_ref[32:32 + tt, :] = u_ref[...]
    acc = jnp.zeros((tt, G), F32)
    for j in range(CONV_W):
        acc = acc + w_ref[j:j + 1, :] * ext_ref[base + j:base + j + tt, :]
    yf = acc + p_ref[0:1, :]
    mu = jnp.mean(yf, axis=-1, keepdims=True)
    d = yf - mu
    var = jnp.mean(d * d, axis=-1, keepdims=True)
    yn = d * lax.rsqrt(var + EPS) * p_ref[1:2, :] + p_ref[2:3, :]
    o_ref[...] = (yn * _sigmoid(yn)).astype(BF16)
    ext_ref[base:32, :] = ext_ref[tt + base:tt + 32, :]


def _conv(u, hist, w, p, *, row_off, b, l, tt):
    nt = l // tt
    off = row_off // tt
    return pl.pallas_call(
        functools.partial(_conv_kernel, tt=tt),
        grid=(b, nt),
        in_specs=[pl.BlockSpec((tt, G), lambda bi, i: (off + bi * nt + i, 0)),
                  pl.BlockSpec((1, HALO, G), lambda bi, i: (bi, 0, 0)),
                  pl.BlockSpec((CONV_W, G), lambda bi, i: (0, 0)),
                  pl.BlockSpec((3, G), lambda bi, i: (0, 0))],
        out_specs=pl.BlockSpec((tt, G), lambda bi, i: (bi * nt + i, 0)),
        out_shape=jax.ShapeDtypeStruct((b * l, G), BF16),
        scratch_shapes=[pltpu.VMEM((tt + 32, G), F32)],
        compiler_params=_params("arbitrary", "arbitrary"),
        name="conv",
    )(u, hist, w, p)


def _lanes(x, w):
    if w < LANES:
        return x[:, :w]
    return x if w == LANES else pltpu.repeat(x, w // LANES, axis=1)


def _scores(q4, k, t_layout):
    return _dot(q4, k) if t_layout else _dot_nt(q4, k)


def _weighted(p4, v, t_layout):
    return _dot_nt(p4, v) if t_layout else _dot(p4, v)


def _fox_kernel(*refs, tq, tkp, n_past, cur_t):
    if n_past:
        (q_ref, kc_ref, vc_ref, ckc_ref, cq_ref, kp_ref, vp_ref, ckp_ref,
         o_ref, m_ref, l_ref, acc_ref) = refs
    else:
        q_ref, kc_ref, vc_ref, ckc_ref, cq_ref, o_ref, m_ref, l_ref, acc_ref = refs
    i = pl.program_id(1)
    lane_head = lax.broadcasted_iota(jnp.int32, (tq, G), 1) >> 6
    q4 = _stack_heads(q_ref[...], lane_head)
    cq = cq_ref[...]
    cqb = [jnp.broadcast_to(cq[:, h:h + 1], (tq, LANES)) for h in range(H)]
    tri = (lax.broadcasted_iota(jnp.int32, (tq, tq), 1)
           <= lax.broadcasted_iota(jnp.int32, (tq, tq), 0))
    m_ref[...] = jnp.full(m_ref.shape, NEG_BIG, F32)
    l_ref[...] = jnp.zeros(l_ref.shape, F32)
    acc_ref[...] = jnp.zeros(acc_ref.shape, F32)

    def step(k, v, ck, masked, t_layout):
        s = _scores(q4, k, t_layout)
        w = s.shape[1]
        ps = []
        for h in range(H):
            sl = slice(h * tq, (h + 1) * tq)
            sh = s[sl] - ck[h:h + 1, :]
            if masked:
                sh = jnp.where(tri, sh, NEG_BIG)
            m_old = m_ref[sl]
            m_new = jnp.maximum(m_old, jnp.max(sh, axis=-1, keepdims=True) + cqb[h])
            p = jnp.exp(sh - _lanes(m_new - cqb[h], w))
            alpha = jnp.exp(m_old - m_new)
            l_ref[sl] = alpha * l_ref[sl] + jnp.sum(p, axis=-1, keepdims=True)
            m_ref[sl] = m_new
            acc_ref[sl] = acc_ref[sl] * _lanes(alpha, G)
            ps.append(p.astype(BF16))
        acc_ref[...] += _weighted(jnp.concatenate(ps, axis=0), v, t_layout)

    for j in range(n_past):
        sl = slice(j * tkp, (j + 1) * tkp)
        step(kp_ref[0, 0, :, sl].astype(BF16), vp_ref[0, 0, :, sl].astype(BF16), ckp_ref[0, j], False, True)

    if cur_t:
        def cur_body(j, c):
            step(kc_ref[0, j], vc_ref[0, j], ckc_ref[0, j], False, True)
            return c
        lax.fori_loop(0, i, cur_body, 0)
        step(kc_ref[0, i], vc_ref[0, i], ckc_ref[0, i], True, True)
    else:
        step(kc_ref[...], vc_ref[...], ckc_ref[0, 0], True, False)

    acc4 = jnp.concatenate(
        [acc_ref[h * tq:(h + 1) * tq] * _lanes(1.0 / l_ref[h * tq:(h + 1) * tq], G) for h in range(H)],
        axis=0)
    o_ref[...] = _unstack_heads(acc4, lane_head, tq).astype(BF16)


def _sb_kernel(*refs, tq, tkp, n_past, cur_t):
    if n_past:
        q_ref, kc_ref, vc_ref, kp_ref, vp_ref, o_ref, r_ref, acc_ref = refs
    else:
        q_ref, kc_ref, vc_ref, o_ref, r_ref, acc_ref = refs
    i = pl.program_id(1)
    lane_head = lax.broadcasted_iota(jnp.int32, (tq, G), 1) >> 6
    q4 = _stack_heads(q_ref[...], lane_head)
    tri = (lax.broadcasted_iota(jnp.int32, (tq, tq), 1)
           < lax.broadcasted_iota(jnp.int32, (tq, tq), 0))
    valid = jnp.concatenate([tri] * H, axis=0)
    r_ref[...] = jnp.zeros(r_ref.shape, F32)
    acc_ref[...] = jnp.zeros(acc_ref.shape, F32)

    def later_matrix(n):
        later = (lax.broadcasted_iota(jnp.int32, (n, n), 0) > lax.broadcasted_iota(jnp.int32, (n, n), 1))
        return jnp.where(later, 1.0, 0.0).astype(BF16)

    widths = {min(tq, SB_BLOCK)} | ({min(tkp, SB_BLOCK)} if n_past else set())
    later = {n: later_matrix(n) for n in widths}

    def step(k, v, masked, t_layout):
        z = _scores(q4, k, t_layout)
        w = z.shape[1]
        sp = jnp.maximum(z, 0.0) + jnp.log(1.0 + jnp.exp(-jnp.abs(z)))
        u = jnp.where(valid, sp, 0.0) if masked else sp
        bw = min(w, SB_BLOCK)
        r_run = r_ref[...]
        rests = [None] * (w // bw)
        for blk in reversed(range(w // bw)):
            ub = u[:, blk * bw:(blk + 1) * bw]
            rests[blk] = _dot(ub.astype(BF16), later[bw]) + _lanes(r_run, bw)
            r_run = r_run + jnp.sum(ub, axis=-1, keepdims=True)
        rest = rests[0] if len(rests) == 1 else jnp.concatenate(rests, axis=1)
        a = jnp.exp(z - sp - rest)
        if masked:
            a = jnp.where(valid, a, 0.0)
        acc_ref[...] += _weighted(a.astype(BF16), v, t_layout)
        r_ref[...] = r_run

    if cur_t:
        step(kc_ref[0, i], vc_ref[0, i], True, True)

        def cur_body(n, c):
            j = i - 1 - n
            step(kc_ref[0, j], vc_ref[0, j], False, True)
            return c
        lax.fori_loop(0, i, cur_body, 0)
    else:
        step(kc_ref[...], vc_ref[...], True, False)

    for j in reversed(range(n_past)):
        sl = slice(j * tkp, (j + 1) * tkp)
        step(kp_ref[0, 0, :, sl].astype(BF16), vp_ref[0, 0, :, sl].astype(BF16), False, True)

    o_ref[...] = _unstack_heads(acc_ref[...], lane_head, tq).astype(BF16)


def _attention(kind, q, kc, vc, extra, past, *, layer, b, l, tq, tkp):
    nq = l // tq
    cur_t = kc.ndim == 4
    assert cur_t or nq == 1
    if cur_t:
        kv_spec = pl.BlockSpec((1, nq, G, tq), lambda bi, i: (bi, 0, 0, 0))
    else:
        kv_spec = pl.BlockSpec((l, G), lambda bi, i: (bi, 0))
    in_specs = [pl.BlockSpec((tq, G), lambda bi, i: (bi * nq + i, 0)), kv_spec, kv_spec]
    args = [q, kc, vc]
    if kind == 'fox':
        ckc, cq = extra
        in_specs += [pl.BlockSpec((1, nq, H, tq), lambda bi, i: (bi, 0, 0, 0)),
                     pl.BlockSpec((tq, H), lambda bi, i: (bi * nq + i, 0))]
        args += [ckc, cq]
    n_past = 0
    if past is not None:
        plen = past[0].shape[3]
        n_past = plen // tkp
        cache_spec = pl.BlockSpec((1, 1, G, plen), lambda bi, i: (layer, bi, 0, 0))
        in_specs += [cache_spec, cache_spec]
        args += [past[0], past[1]]
        if kind == 'fox':
            in_specs.append(pl.BlockSpec((1, n_past, H, tkp), lambda bi, i: (bi, 0, 0, 0)))
            args.append(past[2])
    stat = pltpu.VMEM((H * tq, LANES), F32)
    acc = pltpu.VMEM((H * tq, G), F32)
    body = _fox_kernel if kind == 'fox' else _sb_kernel
    return pl.pallas_call(
        functools.partial(body, tq=tq, tkp=tkp, n_past=n_past, cur_t=cur_t),
        grid=(b, nq),
        in_specs=in_specs,
        out_specs=pl.BlockSpec((tq, G), lambda bi, i: (bi * nq + i, 0)),
        out_shape=jax.ShapeDtypeStruct((b * l, G), BF16),
        scratch_shapes=[stat, stat, acc] if kind == 'fox' else [stat, acc],
        compiler_params=_params("parallel", "arbitrary"),
        name=kind,
    )(*args)


def _hgrn_kernel(*refs, tt, has_init):
    if has_init:
        (q_ref, k_ref, v_ref, lf_ref, gate_ref, hn_ref, st0_ref,
         o_ref, sto_ref, st_ref, kpad, vpad, bpad) = refs
    else:
        (q_ref, k_ref, v_ref, lf_ref, gate_ref, hn_ref,
         o_ref, sto_ref, st_ref, kpad, vpad, bpad) = refs
    i = pl.program_id(1)
    nc = tt // CHUNK

    @pl.when(i == 0)
    def _():
        if has_init:
            st_ref[...] = st0_ref[0]
        else:
            st_ref[...] = jnp.zeros(st_ref.shape, F32)

    q = q_ref[...]
    kk = k_ref[...]
    v = v_ref[...]
    row = lax.broadcasted_iota(jnp.int32, (tt, G), 0)
    r64 = row & (CHUNK - 1)
    r16 = row & (SUB - 1)
    lane_head = lax.broadcasted_iota(jnp.int32, (tt, G), 1) >> 6
    same_head = (lax.broadcasted_iota(jnp.int32, (G, G), 0) >> 6
                 == lax.broadcasted_iota(jnp.int32, (G, G), 1) >> 6)
    bd = jnp.where(same_head, 1.0, 0.0).astype(BF16)

    b = lf_ref[...]
    sh = 1
    while sh < CHUNK:
        b = b + jnp.where(r64 >= sh, pltpu.roll(b, sh, axis=0), 0.0)
        sh *= 2

    def chunk_row(r):
        return jnp.concatenate(
            [jnp.broadcast_to(b[c * CHUNK + r:c * CHUNK + r + 1, :], (CHUNK, G)) for c in range(nc)], axis=0)

    zpad = jnp.zeros((SUB, G), F32)
    kpad[0:SUB, :] = zpad
    vpad[0:SUB, :] = zpad
    bpad[0:SUB, :] = zpad
    kpad[SUB:SUB + tt, :] = kk
    vpad[SUB:SUB + tt, :] = v
    bpad[SUB:SUB + tt, :] = b
    o = jnp.zeros((tt, G), F32)
    for d in range(SUB):
        ks = kpad[SUB - d:SUB - d + tt, :]
        vs = vpad[SUB - d:SUB - d + tt, :]
        bs = bpad[SUB - d:SUB - d + tt, :]
        p = q * ks * jnp.exp(jnp.minimum(b - bs, 0.0))
        p = jnp.where(r16 >= d, p, 0.0)
        o = o + _dot(p.astype(BF16), bd) * vs

    refs_b = [chunk_row(SUB * n - 1) for n in range(1, CHUNK // SUB)]
    sub = r64 >> 4
    rq = b
    for n, rb in enumerate(refs_b, start=1):
        rq = jnp.where(sub == n, rb, rq)
    q4 = _stack_heads((q * jnp.exp(jnp.minimum(b - rq, 0.0))).astype(BF16), lane_head)
    tr = lax.broadcasted_iota(jnp.int32, (tt, tt), 0)
    sc = lax.broadcasted_iota(jnp.int32, (tt, tt), 1)
    tsub = (tr & (CHUNK - 1)) >> 4
    earlier = jnp.where(tr >> 6 == sc >> 6, (sc & (CHUNK - 1)) >> 4, CHUNK) < tsub
    sel = jnp.where(earlier, tsub, 0)
    scores = []
    for rb in refs_b:
        kt = (kk * jnp.exp(jnp.minimum(rb - b, 0.0))).astype(BF16)
        scores.append(_dot_nt(q4, kt))
    a4 = []
    for h in range(H):
        sl = slice(h * tt, (h + 1) * tt)
        a = jnp.zeros((tt, tt), F32)
        for n, s in enumerate(scores, start=1):
            a = jnp.where(sel == n, s[sl], a)
        a4.append(a.astype(BF16))
    vb = v.astype(BF16)
    o = o + _unstack_heads(_dot(jnp.concatenate(a4, axis=0), vb), lane_head, tt)

    blast = chunk_row(CHUNK - 1)
    qs = (q * jnp.exp(b)).astype(BF16)
    kd = (kk * jnp.exp(jnp.minimum(blast - b, 0.0))).astype(BF16)
    dec = jnp.exp(blast)
    outs = []
    for c in range(nc):
        sl = slice(c * CHUNK, (c + 1) * CHUNK)
        st = st_ref[...]
        outs.append(_dot_nt(qs[sl], st.astype(BF16)))
        upd = _dot_tn(vb[sl], kd[sl])
        st_ref[...] = st * dec[c * CHUNK:c * CHUNK + 1, :] + jnp.where(same_head, upd, 0.0)
    o = o + jnp.concatenate(outs, axis=0)

    sq = o * o
    hi = sq.astype(BF16)
    lo = (sq - hi.astype(F32)).astype(BF16)
    ms = (_dot(hi, bd) + _dot(lo, bd)) * (1.0 / HD)
    gate = gate_ref[...]
    o_ref[...] = (o * lax.rsqrt(ms + EPS) * hn_ref[...] * (gate * _sigmoid(gate))).astype(BF16)

    @pl.when(i == pl.num_programs(1) - 1)
    def _():
        sto_ref[0] = st_ref[...]


def _hgrn(q, k, v, lf, gate, hn, st0, *, row_off, b, l, tt):
    nt = l // tt
    off = row_off // tt
    tok = pl.BlockSpec((tt, G), lambda bi, i: (off + bi * nt + i, 0))
    in_specs = [tok, tok, tok, tok, tok, pl.BlockSpec((1, G), lambda bi, i: (0, 0))]
    args = [q, k, v, lf, gate, hn]
    if st0 is not None:
        in_specs.append(pl.BlockSpec((1, G, G), lambda bi, i: (bi, 0, 0)))
        args.append(st0)
    return pl.pallas_call(
        functools.partial(_hgrn_kernel, tt=tt, has_init=st0 is not None),
        grid=(b, nt),
        in_specs=in_specs,
        out_specs=[pl.BlockSpec((tt, G), lambda bi, i: (bi * nt + i, 0)),
                   pl.BlockSpec((1, G, G), lambda bi, i: (bi, 0, 0))],
        out_shape=[jax.ShapeDtypeStruct((b * l, G), BF16), jax.ShapeDtypeStruct((b, G, G), F32)],
        scratch_shapes=[pltpu.VMEM((G, G), F32)] + [pltpu.VMEM((tt + SUB, G), F32)] * 3,
        compiler_params=_params("arbitrary", "arbitrary"),
        name="hgrn",
    )(*args)


def _c1_kernel(x_ref, m0_ref, m1_ref, m2_ref, m3_ref, wo_ref, g_ref, wq_ref, x1_ref, q_ref):
    acc = x_ref[...]
    for p, m_ref in enumerate((m0_ref, m1_ref, m2_ref, m3_ref)):
        acc = acc + _dot(m_ref[...], wo_ref[p * G:(p + 1) * G, :])
    x1_ref[...] = acc
    h = _rms(acc, g_ref[...]).astype(BF16)
    q_ref[...] = (_dot(h, wq_ref[...]) * (MEM_HD ** -0.5)).astype(BF16)


def _c1(x, mix, wo, g, wq, tt):
    t = x.shape[0]
    row = lambda i: (i, 0)
    fix = lambda i: (0, 0)
    return pl.pallas_call(
        _c1_kernel,
        grid=(t // tt,),
        in_specs=[pl.BlockSpec((tt, D_MODEL), row)] + [pl.BlockSpec((tt, G), row)] * 4
                 + [pl.BlockSpec((D_MODEL, D_MODEL), fix), pl.BlockSpec((1, D_MODEL), fix),
                    pl.BlockSpec((D_MODEL, D_MODEL), fix)],
        out_specs=[pl.BlockSpec((tt, D_MODEL), row), pl.BlockSpec((tt, D_MODEL), row)],
        out_shape=[jax.ShapeDtypeStruct((t, D_MODEL), F32), jax.ShapeDtypeStruct((t, D_MODEL), BF16)],
        compiler_params=_params("parallel"),
        name="outproj_memq",
    )(x, *mix, wo, g, wq)


def _memkv_kernel(m_ref, wk_ref, wv_ref, k_ref, kb_ref, v_ref, vb_ref):
    m = m_ref[...].astype(BF16)
    k = _dot(m, wk_ref[...])
    k_ref[...] = k
    kb_ref[...] = k.astype(BF16)
    v = _dot(m, wv_ref[...])
    v_ref[...] = v
    vb_ref[...] = v.astype(BF16)


def _memkv(mem, wk, wv, tt):
    t = mem.shape[0]
    row = lambda i: (i, 0)
    fix = lambda i: (0, 0)
    f32o = jax.ShapeDtypeStruct((t, D_MODEL), F32)
    b16o = jax.ShapeDtypeStruct((t, D_MODEL), BF16)
    return pl.pallas_call(
        _memkv_kernel,
        grid=(t // tt,),
        in_specs=[pl.BlockSpec((tt, D_MODEL), row), pl.BlockSpec((D_MODEL, D_MODEL), fix),
                  pl.BlockSpec((D_MODEL, D_MODEL), fix)],
        out_specs=[pl.BlockSpec((tt, D_MODEL), row)] * 4,
        out_shape=[f32o, b16o, f32o, b16o],
        compiler_params=_params("parallel"),
        name="memkv",
    )(mem, wk, wv)


def _memattn_kernel(q_ref, k_ref, v_ref, o_ref):
    outs = []
    for h in range(MEM_HEADS):
        sl = slice(h * MEM_HD, (h + 1) * MEM_HD)
        s = _dot_nt(q_ref[:, sl], k_ref[:, sl].astype(BF16))
        p = jnp.exp(s - jnp.max(s, axis=-1, keepdims=True))
        den = jnp.sum(p, axis=-1, keepdims=True)
        outs.append(_dot(p.astype(BF16), v_ref[:, sl].astype(BF16)) * (1.0 / den))
    o_ref[...] = jnp.concatenate(outs, axis=-1).astype(BF16)


def _memattn(q, k, v, *, row_off, b, l, tq):
    nq = l // tq
    off = row_off // tq
    return pl.pallas_call(
        _memattn_kernel,
        grid=(b, nq),
        in_specs=[pl.BlockSpec((tq, D_MODEL), lambda bi, i: (off + bi * nq + i, 0)),
                  pl.BlockSpec((N_MEM, D_MODEL), lambda bi, i: (bi, 0)),
                  pl.BlockSpec((N_MEM, D_MODEL), lambda bi, i: (bi, 0))],
        out_specs=pl.BlockSpec((tq, D_MODEL), lambda bi, i: (bi * nq + i, 0)),
        out_shape=jax.ShapeDtypeStruct((b * l, D_MODEL), BF16),
        compiler_params=_params("parallel", "parallel"),
        name="memattn",
    )(q, k, v)


def _c3_kernel(x1_ref, o_ref, wmo_ref, g_ref, wup_ref, wdn_ref, gf_ref, y_ref,
               x2_ref, h_ref, acc_ref, *, final):
    f = pl.program_id(1)

    @pl.when(f == 0)
    def _():
        x2 = x1_ref[...] + _dot(o_ref[...], wmo_ref[...])
        x2_ref[...] = x2
        h_ref[...] = _rms(x2, g_ref[...]).astype(BF16)
        acc_ref[...] = jnp.zeros(acc_ref.shape, F32)

    a = jnp.maximum(_dot(h_ref[...], wup_ref[...]), 0.0)
    acc_ref[...] += _dot((a * a).astype(BF16), wdn_ref[...])

    @pl.when(f == pl.num_programs(1) - 1)
    def _():
        x3 = x2_ref[...] + acc_ref[...]
        y_ref[...] = _rms(x3, gf_ref[...]) if final else x3


def _c3(x1, o, wmo, g, wup, wdn, gf, *, tt, tf, final):
    t = x1.shape[0]
    row = lambda i, f: (i, 0)
    fix = lambda i, f: (0, 0)
    return pl.pallas_call(
        functools.partial(_c3_kernel, final=final),
        grid=(t // tt, D_FF // tf),
        in_specs=[pl.BlockSpec((tt, D_MODEL), row), pl.BlockSpec((tt, D_MODEL), row),
                  pl.BlockSpec((D_MODEL, D_MODEL), fix), pl.BlockSpec((1, D_MODEL), fix),
                  pl.BlockSpec((D_MODEL, tf), lambda i, f: (0, f)),
                  pl.BlockSpec((tf, D_MODEL), lambda i, f: (f, 0)),
                  pl.BlockSpec((1, D_MODEL), fix)],
        out_specs=pl.BlockSpec((tt, D_MODEL), row),
        out_shape=jax.ShapeDtypeStruct((t, D_MODEL), F32),
        scratch_shapes=[pltpu.VMEM((tt, D_MODEL), F32), pltpu.VMEM((tt, D_MODEL), BF16),
                        pltpu.VMEM((tt, D_MODEL), F32)],
        compiler_params=_params("parallel", "arbitrary"),
        name="memout_mlp",
    )(x1, o, wmo, g, wup, wdn, gf)


def _block_diag_t(s):
    b = s.shape[0]
    eye = jnp.eye(H, dtype=s.dtype)
    st = jnp.swapaxes(s, 2, 3)
    return jnp.einsum('bhvk,hg->bhvgk', st, eye).reshape(b, G, G)


def _unblock_diag_t(st):
    b = st.shape[0]
    s5 = st.reshape(b, H, HD, H, HD)
    d = jnp.stack([s5[:, h, :, h, :] for h in range(H)], axis=1)
    return jnp.swapaxes(d, 2, 3)


def _row_cumsum(lf, tile):
    b, s, _ = lf.shape
    sp = -(-s // tile) * tile
    x = jnp.swapaxes(lf, 1, 2).reshape(b * H, s)
    x = jnp.pad(x, ((0, 0), (0, sp - s)))
    return _cumsum_lanes(x).reshape(b, H, sp)[:, :, :s]


def _tiles(c, n, t):
    b = c.shape[0]
    return jnp.swapaxes(c.reshape(b, H, n, t), 1, 2)


def kernel(x_prompt, x_sample, mem_prompt, cache_conv, cache_fox_k, cache_fox_v, cache_fox_logf,
           state_hgrn, cache_sb_k, cache_sb_v, cache_mem_k, cache_mem_v, norm_mix, w_in, b_fox_f,
           conv_w, conv_b, conv_ln_g, conv_ln_b, hgrn_lb, hgrn_norm, w_out, norm_mem, w_mq, w_mk,
           w_mv, w_mo, norm_ffn, w_up, w_down, norm_final):
    bp, sp, _ = x_prompt.shape
    bs, ls, _ = x_sample.shape
    depth = w_in.shape[0]
    past = cache_fox_k.shape[2]
    tp = bp * sp
    ts = bs * ls
    tok = 512
    tq_p = 512
    tt_h = 256
    tk_past = 512

    sm = jax.nn.softmax(hgrn_lb.astype(F32), axis=0)
    lower = jnp.clip(jnp.cumsum(sm, axis=0) - sm[0], 0.0, 1.0 - 1e-6)
    pos = lower > 0.0
    hconst = jnp.stack([jnp.log1p(-lower), jnp.log(jnp.where(pos, lower, 1.0)),
                        pos.astype(F32), 1.0 - lower], axis=1)

    o_ff = 5 * G
    w_in_r = jnp.concatenate(
        [w_in[:, :, :o_ff], w_in[:, :, o_ff + H:],
         jnp.pad(w_in[:, :, o_ff:o_ff + H], ((0, 0), (0, 0), (0, FF_PAD - H)))], axis=-1).astype(BF16)
    bff = jnp.pad(b_fox_f, ((0, 0), (0, FF_PAD - H)))[:, None, :]
    w_out_b = w_out.astype(BF16)
    w_mq_b = w_mq.astype(BF16)
    w_mk_b = w_mk.astype(BF16)
    w_mv_b = w_mv.astype(BF16)
    w_mo_b = w_mo.astype(BF16)
    w_up_b = w_up.astype(BF16)
    w_down_b = w_down.astype(BF16)
    hn = jnp.tile(hgrn_norm, (1, H))[:, None, :]
    conv_p3 = jnp.stack([conv_b, conv_ln_g, conv_ln_b], axis=1)

    def seq_last(c):
        return jnp.transpose(c, (0, 1, 3, 4, 2)).reshape(depth, bs, G, past)

    fkc, fvc, skc, svc = (seq_last(c) for c in (cache_fox_k, cache_fox_v, cache_sb_k, cache_sb_v))

    xp = x_prompt.reshape(tp, D_MODEL)
    xs = x_sample.reshape(ts, D_MODEL)
    mem2 = mem_prompt.reshape(bp * N_MEM, D_MODEL)
    zero_hist = jnp.zeros((bp, HALO, G), F32)
    bufs = None

    outs = {n: [] for n in ('conv_p', 'flf_p', 'hg_p', 'mk_p', 'mv_p',
                            'conv_s', 'fk_s', 'fv_s', 'flf_s', 'hg_s', 'sk_s', 'sv_s')}
    for l in range(depth):
        pp = _proj_seq(xp, norm_mix[l][None], w_in_r[l], bff[l], hconst[l], bufs,
                       layer=l, depth=depth, b=bp, l=sp, tt=tq_p)
        bufs = tuple(pp[n + '_t'] for n in _PROJ_KV)
        ps = _proj_rows(xs, norm_mix[l][None], w_in_r[l], bff[l], hconst[l], tok)

        conv_op = _conv(pp['u'], zero_hist, conv_w[l], conv_p3[l], row_off=0, b=bp, l=sp, tt=tok)
        conv_os = _conv(ps['u'], cache_conv[l], conv_w[l], conv_p3[l], row_off=0, b=bs, l=ls, tt=ls)

        flf_pt = pp['flf'][:, :H, :]
        flf_s = ps['flf'][:, :H].reshape(bs, ls, H)
        c_p = _cumsum_lanes(flf_pt.reshape(bp * H, sp)).reshape(bp, H, sp)
        c_s = _row_cumsum(jnp.concatenate([cache_fox_logf[l].astype(F32), flf_s], axis=1), LANES)
        fox_op = _attention('fox', pp['fq'], pp['fk_tb'], pp['fv_tb'],
                            (_tiles(c_p, sp // tq_p, tq_p), jnp.swapaxes(c_p, 1, 2).reshape(tp, H)),
                            None, layer=l, b=bp, l=sp, tq=tq_p, tkp=tk_past)
        fox_os = _attention('fox', ps['fq'], ps['fk_b'], ps['fv_b'],
                            (_tiles(c_s[:, :, past:], 1, ls), jnp.swapaxes(c_s[:, :, past:], 1, 2).reshape(ts, H)),
                            (fkc, fvc, _tiles(c_s[:, :, :past], past // tk_past, tk_past)),
                            layer=l, b=bs, l=ls, tq=ls, tkp=tk_past)

        hg_op, st_p = _hgrn(pp['hq'], pp['hk'], pp['hv'], pp['hlf'], pp['hgate'], hn[l], None,
                            row_off=0, b=bp, l=sp, tt=tt_h)
        hg_os, st_s = _hgrn(ps['hq'], ps['hk'], ps['hv'], ps['hlf'], ps['hgate'], hn[l],
                            _block_diag_t(state_hgrn[l].astype(F32)), row_off=0, b=bs, l=ls, tt=ls)

        sb_op = _attention('sb', pp['sq'], pp['sk_tb'], pp['sv_tb'], (), None,
                           layer=l, b=bp, l=sp, tq=tq_p, tkp=tk_past)
        sb_os = _attention('sb', ps['sq'], ps['sk_b'], ps['sv_b'], (), (skc, svc),
                           layer=l, b=bs, l=ls, tq=ls, tkp=tk_past)

        x1p, qmp = _c1(xp, [conv_op, fox_op, hg_op, sb_op], w_out_b[l], norm_mem[l][None], w_mq_b[l], tok)
        x1s, qms = _c1(xs, [conv_os, fox_os, hg_os, sb_os], w_out_b[l], norm_mem[l][None], w_mq_b[l], tok)

        mk, mkb, mv, mvb = _memkv(mem2, w_mk_b[l], w_mv_b[l], N_MEM)
        om_p = _memattn(qmp, mkb, mvb, row_off=0, b=bp, l=sp, tq=tok)
        om_s = _memattn(qms, cache_mem_k[l].reshape(bs * N_MEM, D_MODEL),
                        cache_mem_v[l].reshape(bs * N_MEM, D_MODEL), row_off=0, b=bs, l=ls, tq=ls)

        final = l == depth - 1
        xp = _c3(x1p, om_p, w_mo_b[l], norm_ffn[l][None], w_up_b[l], w_down_b[l], norm_final[None],
                 tt=tok, tf=1024, final=final)
        xs = _c3(x1s, om_s, w_mo_b[l], norm_ffn[l][None], w_up_b[l], w_down_b[l], norm_final[None],
                 tt=tok, tf=1024, final=final)

        u_s = jnp.concatenate([cache_conv[l].astype(F32), ps['u'].reshape(bs, ls, G)], axis=1)
        outs['conv_p'].append(pp['u'].reshape(bp, sp, G)[:, sp - HALO:])
        outs['conv_s'].append(u_s[:, ls:])
        for name in _PROJ_KV:
            outs[name + '_s'].append(ps[name].reshape(bs, ls, H, HD))
        outs['flf_p'].append(flf_pt)
        outs['flf_s'].append(flf_s)
        outs['hg_p'].append(_unblock_diag_t(st_p))
        outs['hg_s'].append(_unblock_diag_t(st_s))
        outs['mk_p'].append(mk.reshape(bp, N_MEM, MEM_HEADS, MEM_HD))
        outs['mv_p'].append(mv.reshape(bp, N_MEM, MEM_HEADS, MEM_HD))

    st = lambda n: jnp.stack(outs[n])
    kv_p = [jnp.transpose(t.reshape(depth, bp, H, HD, sp), (0, 1, 4, 2, 3)) for t in bufs]
    return (xp.reshape(bp, sp, D_MODEL), xs.reshape(bs, ls, D_MODEL),
            st('conv_p'), kv_p[0], kv_p[1], jnp.swapaxes(st('flf_p'), 2, 3), st('hg_p'),
            kv_p[2], kv_p[3], st('mk_p'), st('mv_p'),
            st('conv_s'), st('fk_s'), st('fv_s'), st('flf_s'), st('hg_s'),
            st('sk_s'), st('sv_s'))
```

```python
import functools

import jax
import jax.numpy as jnp
from jax import lax
from jax.experimental import pallas as pl
from jax.experimental.pallas import tpu as pltpu

F32 = jnp.float32
BF16 = jnp.bfloat16

D_MODEL = 1024
G = 256
H = 4
HD = 64
CONV_W = 31
HALO = CONV_W - 1
N_MEM = 256
MEM_HEADS = 4
MEM_HD = 256
D_FF = 4096
EPS = 1e-6
NEG_BIG = -1e30
SUB = 8
CHUNK = 64
SUB_SHIFT = SUB.bit_length() - 1
CHUNK_SHIFT = CHUNK.bit_length() - 1
HD_SHIFT = HD.bit_length() - 1
FF_PAD = 128
LANES = 128
SB_BLOCK = 256
ATTN_GROUPS = 8
ATTN_GROUP_MIN_ROWS = 256
LOG2E = 1.4426950408889634
SOFTPLUS_LINEAR = 64.0
QK_SCALE = HD ** -0.5 * LOG2E
VMEM_LIMIT_BYTES = 52 * 1024 * 1024


def _params(*sem):
    return pltpu.CompilerParams(dimension_semantics=sem, vmem_limit_bytes=VMEM_LIMIT_BYTES)


def _logsig(x):
    return jnp.minimum(x, 0.0) - jnp.log(1.0 + jnp.exp(-jnp.abs(x)))


def _sigmoid(x):
    return 1.0 / (1.0 + jnp.exp(-x))


def _rms(x, g):
    return x * lax.rsqrt(jnp.mean(x * x, axis=-1, keepdims=True) + EPS) * g


def _dot(a, b):
    return jnp.dot(a, b, preferred_element_type=F32)


def _dot_nt(a, b):
    return lax.dot_general(a, b, (((1,), (1,)), ((), ())), preferred_element_type=F32)


def _dot_tn(a, b):
    return lax.dot_general(a, b, (((0,), (0,)), ((), ())), preferred_element_type=F32)


def _stack_heads(x, lane_head):
    return jnp.concatenate([jnp.where(lane_head == h, x, jnp.zeros_like(x)) for h in range(H)], axis=0)


def _unstack_heads(x4, lane_head, t):
    out = jnp.zeros((t, G), x4.dtype)
    for h in range(H):
        out = jnp.where(lane_head == h, x4[h * t:(h + 1) * t], out)
    return out


def _proj_groups(x_ref, g_ref, w_ref, bff_ref, hc_ref):
    h = _rms(x_ref[...], g_ref[...]).astype(BF16)

    def col(i, n=G):
        return _dot(h, w_ref[:, i * G:i * G + n])

    yield 'u', col(0) * _sigmoid(col(1))
    yield 'fq', col(2) * QK_SCALE
    yield 'fk', col(3)
    yield 'fv', col(4)
    yield 'hq', col(5)
    hz = col(6)
    base = hc_ref[0:1, :] + _logsig(hz)
    c1 = hc_ref[1:2, :]
    lae = jnp.maximum(c1, base) + jnp.log(1.0 + jnp.exp(-jnp.abs(c1 - base)))
    yield 'hlf', jnp.where(hc_ref[2:3, :] > 0.5, lae, base)
    yield 'hk', hc_ref[3:4, :] * _sigmoid(-hz)
    yield 'hv', col(7)
    yield 'hgate', col(8)
    yield 'sq', col(9) * QK_SCALE
    yield 'sk', col(10)
    yield 'sv', col(11)
    yield 'flf', _logsig(col(12, FF_PAD) + bff_ref[...])


_PROJ_ROW_F32 = ('u', 'hq', 'hlf', 'hk', 'hv', 'hgate')
_PROJ_ROW_BF16 = ('fq', 'sq')
_PROJ_KV = ('fk', 'fv', 'sk', 'sv')


def _proj_rows_kernel(x_ref, g_ref, w_ref, bff_ref, hc_ref, *out_refs):
    names = _PROJ_ROW_F32 + _PROJ_ROW_BF16 + ('flf',) + tuple(n + s for n in _PROJ_KV for s in ('', '_b'))
    refs = dict(zip(names, out_refs))
    for name, val in _proj_groups(x_ref, g_ref, w_ref, bff_ref, hc_ref):
        if name in _PROJ_KV:
            refs[name][...] = val
            refs[name + '_b'][...] = val.astype(BF16)
        else:
            refs[name][...] = val.astype(refs[name].dtype)


def _proj_rows(x, g, w, bff, hc, tt):
    t = x.shape[0]
    row = lambda i: (i, 0)
    fix = lambda i: (0, 0)
    f32o = jax.ShapeDtypeStruct((t, G), F32)
    b16o = jax.ShapeDtypeStruct((t, G), BF16)
    outs = ([f32o] * len(_PROJ_ROW_F32) + [b16o] * len(_PROJ_ROW_BF16)
            + [jax.ShapeDtypeStruct((t, FF_PAD), F32)] + [f32o, b16o] * len(_PROJ_KV))
    names = _PROJ_ROW_F32 + _PROJ_ROW_BF16 + ('flf',) + tuple(n + s for n in _PROJ_KV for s in ('', '_b'))
    res = pl.pallas_call(
        _proj_rows_kernel,
        grid=(t // tt,),
        in_specs=[pl.BlockSpec((tt, D_MODEL), row), pl.BlockSpec((1, D_MODEL), fix),
                  pl.BlockSpec((D_MODEL, w.shape[1]), fix), pl.BlockSpec((1, FF_PAD), fix),
                  pl.BlockSpec((4, G), fix)],
        out_specs=[pl.BlockSpec((tt, o.shape[1]), row) for o in outs],
        out_shape=outs,
        compiler_params=_params("parallel"),
        name="proj_rows",
    )(x, g, w, bff, hc)
    return dict(zip(names, res))


def _proj_seq_kernel(x_ref, g_ref, w_ref, bff_ref, hc_ref, *rest, n_alias):
    out_refs = rest[n_alias:]
    names = (_PROJ_ROW_F32 + _PROJ_ROW_BF16 + ('flf',)
             + tuple(n + s for n in _PROJ_KV for s in ('_t', '_tb')))
    refs = dict(zip(names, out_refs))
    for name, val in _proj_groups(x_ref, g_ref, w_ref, bff_ref, hc_ref):
        if name in _PROJ_KV:
            vt = val.T
            refs[name + '_t'][0, 0] = vt
            refs[name + '_tb'][0, 0] = vt.astype(BF16)
        elif name == 'flf':
            refs[name][0] = val.T[0:8, :]
        else:
            refs[name][...] = val.astype(refs[name].dtype)


def _proj_seq(x, g, w, bff, hc, bufs, *, layer, depth, b, l, tt):
    t = x.shape[0]
    nt = l // tt
    row = lambda i: (i, 0)
    fix = lambda i: (0, 0)
    f32o = jax.ShapeDtypeStruct((t, G), F32)
    b16o = jax.ShapeDtypeStruct((t, G), BF16)
    stacked = jax.ShapeDtypeStruct((depth, b, G, l), F32)
    tiled = jax.ShapeDtypeStruct((b, nt, G, tt), BF16)
    outs = ([f32o] * len(_PROJ_ROW_F32) + [b16o] * len(_PROJ_ROW_BF16)
            + [jax.ShapeDtypeStruct((b, 8, l), F32)] + [stacked, tiled] * len(_PROJ_KV))
    ospecs = ([pl.BlockSpec((tt, G), row)] * (len(_PROJ_ROW_F32) + len(_PROJ_ROW_BF16))
              + [pl.BlockSpec((1, 8, tt), lambda i: (i // nt, 0, i % nt))]
              + [pl.BlockSpec((1, 1, G, tt), lambda i: (layer, i // nt, 0, i % nt)),
                 pl.BlockSpec((1, 1, G, tt), lambda i: (i // nt, i % nt, 0, 0))] * len(_PROJ_KV))
    names = (_PROJ_ROW_F32 + _PROJ_ROW_BF16 + ('flf',)
             + tuple(n + s for n in _PROJ_KV for s in ('_t', '_tb')))
    n_in = 5
    first_stacked = len(_PROJ_ROW_F32) + len(_PROJ_ROW_BF16) + 1
    aliases = {}
    alias_args = []
    if bufs is not None:
        alias_args = list(bufs)
        aliases = {n_in + k: first_stacked + 2 * k for k in range(len(_PROJ_KV))}
    res = pl.pallas_call(
        functools.partial(_proj_seq_kernel, n_alias=len(alias_args)),
        grid=(t // tt,),
        in_specs=[pl.BlockSpec((tt, D_MODEL), row), pl.BlockSpec((1, D_MODEL), fix),
                  pl.BlockSpec((D_MODEL, w.shape[1]), fix), pl.BlockSpec((1, FF_PAD), fix),
                  pl.BlockSpec((4, G), fix)] + [pl.BlockSpec(memory_space=pl.ANY)] * len(alias_args),
        out_specs=ospecs,
        out_shape=outs,
        input_output_aliases=aliases,
        compiler_params=_params("parallel"),
        name="proj_seq",
    )(x, g, w, bff, hc, *alias_args)
    return dict(zip(names, res))


def _cumsum_kernel(x_ref, o_ref, *, scale):
    x = x_ref[...]
    n = x.shape[1]
    lane = lax.broadcasted_iota(jnp.int32, x.shape, 1)
    sh = 1
    while sh < n:
        x = x + jnp.where(lane >= sh, pltpu.roll(x, sh, axis=1), 0.0)
        sh *= 2
    o_ref[...] = x * scale


def _cumsum_lanes(x, scale):
    return pl.pallas_call(
        functools.partial(_cumsum_kernel, scale=scale),
        out_shape=jax.ShapeDtypeStruct(x.shape, F32),
        compiler_params=pltpu.CompilerParams(vmem_limit_bytes=VMEM_LIMIT_BYTES),
        name="cumsum",
    )(x)


def _conv_kernel(u_ref, hist_ref, w_ref, p_ref, o_ref, ext_ref, *, tt):
    i = pl.program_id(1)
    base = 32 - HALO

    @pl.when(i == 0)
    def _():
        ext_ref[0:base, :] = jnp.zeros((base, G), F32)
        ext_ref[base:32, :] = hist_ref[0]
        ext_ref[32 + tt:40 + tt, :] = jnp.zeros((8, G), F32)

    ext_ref[32:32 + tt, :] = u_ref[...]
    acc = None
    for r in range(8):
        part = None
        for m in range((base + CONV_W - 1) // 8 + 1):
            j = 8 * m + r - base
            if 0 <= j < CONV_W:
                term = w_ref[j:j + 1, :] * ext_ref[8 * m:8 * m + tt + 8, :]
                part = term if part is None else part + term
        part = part[r:r + tt, :]
        acc = part if acc is None else acc + part
    yf = acc + p_ref[0:1, :]
    mu = jnp.mean(yf, axis=-1, keepdims=True)
    d = yf - mu
    var = jnp.mean(d * d, axis=-1, keepdims=True)
    yn = d * lax.rsqrt(var + EPS) * p_ref[1:2, :] + p_ref[2:3, :]
    o_ref[...] = (yn * _sigmoid(yn)).astype(BF16)
    ext_ref[base:32, :] = ext_ref[tt + base:tt + 32, :]


def _conv(u, hist, w, p, *, row_off, b, l, tt):
    nt = l // tt
    off = row_off // tt
    return pl.pallas_call(
        functools.partial(_conv_kernel, tt=tt),
        grid=(b, nt),
        in_specs=[pl.BlockSpec((tt, G), lambda bi, i: (off + bi * nt + i, 0)),
                  pl.BlockSpec((1, HALO, G), lambda bi, i: (bi, 0, 0)),
                  pl.BlockSpec((CONV_W, G), lambda bi, i: (0, 0)),
                  pl.BlockSpec((3, G), lambda bi, i: (0, 0))],
        out_specs=pl.BlockSpec((tt, G), lambda bi, i: (bi * nt + i, 0)),
        out_shape=jax.ShapeDtypeStruct((b * l, G), BF16),
        scratch_shapes=[pltpu.VMEM((tt + 40, G), F32)],
        compiler_params=_params("arbitrary", "arbitrary"),
        name="conv",
    )(u, hist, w, p)


def _lanes(x, w):
    if w < LANES:
        return x[:, :w]
    return x if w == LANES else jnp.concatenate([x] * (w // LANES), axis=1)


def _scores(q4, k, t_layout):
    return _dot(q4, k) if t_layout else _dot_nt(q4, k)


def _weighted(p4, v, t_layout):
    return _dot_nt(p4, v) if t_layout else _dot(p4, v)


def _fox_kernel(*refs, tq, tkp, n_past, cur_t):
    if n_past:
        (q_ref, kc_ref, vc_ref, ckc_ref, cq_ref, kp_ref, vp_ref, ckp_ref,
         o_ref, m_ref, l_ref, acc_ref) = refs
    else:
        q_ref, kc_ref, vc_ref, ckc_ref, cq_ref, o_ref, m_ref, l_ref, acc_ref = refs
    i = pl.program_id(1)
    lane_head = lax.broadcasted_iota(jnp.int32, (tq, G), 1) >> HD_SHIFT
    q4 = _stack_heads(q_ref[...], lane_head)
    cq = cq_ref[...]
    cqb = [jnp.broadcast_to(cq[:, h:h + 1], (tq, LANES)) for h in range(H)]
    tri = (lax.broadcasted_iota(jnp.int32, (tq, tq), 1)
           <= lax.broadcasted_iota(jnp.int32, (tq, tq), 0))
    m_ref[...] = jnp.full(m_ref.shape, NEG_BIG, F32)
    l_ref[...] = jnp.zeros(l_ref.shape, F32)
    acc_ref[...] = jnp.zeros(acc_ref.shape, F32)

    def step(k, v, ck, masked, t_layout):
        s = _scores(q4, k, t_layout)
        w = s.shape[1]
        ps = []
        for h in range(H):
            sl = slice(h * tq, (h + 1) * tq)
            sh = s[sl] - ck[h:h + 1, :]
            if masked:
                sh = jnp.where(tri, sh, NEG_BIG)
            m_old = m_ref[sl]
            m_new = jnp.maximum(m_old, jnp.max(sh, axis=-1, keepdims=True) + cqb[h])
            p = jnp.exp2(sh - _lanes(m_new - cqb[h], w))
            alpha = jnp.exp2(m_old - m_new)
            l_ref[sl] = alpha * l_ref[sl] + jnp.sum(p, axis=-1, keepdims=True)
            m_ref[sl] = m_new
            acc_ref[sl] = acc_ref[sl] * _lanes(alpha, G)
            ps.append(p.astype(BF16))
        acc_ref[...] += _weighted(jnp.concatenate(ps, axis=0), v, t_layout)

    for j in range(n_past):
        sl = slice(j * tkp, (j + 1) * tkp)
        step(kp_ref[0, 0, :, sl].astype(BF16), vp_ref[0, 0, :, sl].astype(BF16), ckp_ref[0, j], False, True)

    if cur_t:
        def cur_body(j, c):
            step(kc_ref[0, j], vc_ref[0, j], ckc_ref[0, j], False, True)
            return c
        lax.fori_loop(0, i, cur_body, 0)
        step(kc_ref[0, i], vc_ref[0, i], ckc_ref[0, i], True, True)
    else:
        step(kc_ref[...], vc_ref[...], ckc_ref[0, 0], True, False)

    acc4 = jnp.concatenate(
        [acc_ref[h * tq:(h + 1) * tq] * _lanes(1.0 / l_ref[h * tq:(h + 1) * tq], G) for h in range(H)],
        axis=0)
    o_ref[...] = _unstack_heads(acc4, lane_head, tq).astype(BF16)


def _sb_kernel(*refs, tq, tkp, n_past, cur_t):
    if n_past:
        q_ref, kc_ref, vc_ref, kp_ref, vp_ref, o_ref, r_ref, acc_ref = refs
    else:
        q_ref, kc_ref, vc_ref, o_ref, r_ref, acc_ref = refs
    i = pl.program_id(1)
    lane_head = lax.broadcasted_iota(jnp.int32, (tq, G), 1) >> HD_SHIFT
    q4 = _stack_heads(q_ref[...], lane_head)
    tri = (lax.broadcasted_iota(jnp.int32, (tq, tq), 1)
           < lax.broadcasted_iota(jnp.int32, (tq, tq), 0))
    valid = jnp.concatenate([tri] * H, axis=0)
    r_ref[...] = jnp.zeros(r_ref.shape, F32)
    acc_ref[...] = jnp.zeros(acc_ref.shape, F32)

    def later_matrix(n):
        later = (lax.broadcasted_iota(jnp.int32, (n, n), 0) > lax.broadcasted_iota(jnp.int32, (n, n), 1))
        return jnp.where(later, 1.0, 0.0).astype(BF16)

    widths = {min(tq, SB_BLOCK)} | ({min(tkp, SB_BLOCK)} if n_past else set())
    later = {n: later_matrix(n) for n in widths}

    groups = ATTN_GROUPS if H * tq // ATTN_GROUPS >= ATTN_GROUP_MIN_ROWS else 1

    def step(k, v, masked, t_layout):
        for g in range(groups):
            rows = slice(g * (H * tq // groups), (g + 1) * (H * tq // groups))
            z = _scores(q4[rows], k, t_layout)
            w = z.shape[1]
            sp = jnp.where(z > SOFTPLUS_LINEAR, z, jnp.log2(1.0 + jnp.exp2(z)))
            u = jnp.where(valid[rows], sp, 0.0) if masked else sp
            bw = min(w, SB_BLOCK)
            r_run = r_ref[rows]
            rests = [None] * (w // bw)
            for blk in reversed(range(w // bw)):
                ub = u[:, blk * bw:(blk + 1) * bw]
                rests[blk] = _dot(ub.astype(BF16), later[bw]) + _lanes(r_run, bw)
                r_run = r_run + jnp.sum(ub, axis=-1, keepdims=True)
            rest = rests[0] if len(rests) == 1 else jnp.concatenate(rests, axis=1)
            a = jnp.exp2(z - sp - rest)
            if masked:
                a = jnp.where(valid[rows], a, 0.0)
            acc_ref[rows] += _weighted(a.astype(BF16), v, t_layout)
            r_ref[rows] = r_run

    if cur_t:
        step(kc_ref[0, i], vc_ref[0, i], True, True)

        def cur_body(n, c):
            j = i - 1 - n
            step(kc_ref[0, j], vc_ref[0, j], False, True)
            return c
        lax.fori_loop(0, i, cur_body, 0)
    else:
        step(kc_ref[...], vc_ref[...], True, False)

    for j in reversed(range(n_past)):
        sl = slice(j * tkp, (j + 1) * tkp)
        step(kp_ref[0, 0, :, sl].astype(BF16), vp_ref[0, 0, :, sl].astype(BF16), False, True)

    o_ref[...] = _unstack_heads(acc_ref[...], lane_head, tq).astype(BF16)


def _attention(kind, q, kc, vc, extra, past, *, layer, b, l, tq, tkp):
    nq = l // tq
    cur_t = kc.ndim == 4
    assert cur_t or nq == 1
    if cur_t:
        kv_spec = pl.BlockSpec((1, nq, G, tq), lambda bi, i: (bi, 0, 0, 0))
    else:
        kv_spec = pl.BlockSpec((l, G), lambda bi, i: (bi, 0))
    in_specs = [pl.BlockSpec((tq, G), lambda bi, i: (bi * nq + i, 0)), kv_spec, kv_spec]
    args = [q, kc, vc]
    if kind == 'fox':
        ckc, cq = extra
        in_specs += [pl.BlockSpec((1, nq, H, tq), lambda bi, i: (bi, 0, 0, 0)),
                     pl.BlockSpec((tq, H), lambda bi, i: (bi * nq + i, 0))]
        args += [ckc, cq]
    n_past = 0
    if past is not None:
        plen = past[0].shape[3]
        n_past = plen // tkp
        cache_spec = pl.BlockSpec((1, 1, G, plen), lambda bi, i: (layer, bi, 0, 0))
        in_specs += [cache_spec, cache_spec]
        args += [past[0], past[1]]
        if kind == 'fox':
            in_specs.append(pl.BlockSpec((1, n_past, H, tkp), lambda bi, i: (bi, 0, 0, 0)))
            args.append(past[2])
    stat = pltpu.VMEM((H * tq, LANES), F32)
    acc = pltpu.VMEM((H * tq, G), F32)
    body = _fox_kernel if kind == 'fox' else _sb_kernel
    return pl.pallas_call(
        functools.partial(body, tq=tq, tkp=tkp, n_past=n_past, cur_t=cur_t),
        grid=(b, nq),
        in_specs=in_specs,
        out_specs=pl.BlockSpec((tq, G), lambda bi, i: (bi * nq + i, 0)),
        out_shape=jax.ShapeDtypeStruct((b * l, G), BF16),
        scratch_shapes=[stat, stat, acc] if kind == 'fox' else [stat, acc],
        compiler_params=_params("parallel", "arbitrary"),
        name=kind,
    )(*args)


def _hgrn_kernel(*refs, tt, has_init):
    if has_init:
        (q_ref, k_ref, v_ref, lf_ref, gate_ref, hn_ref, st0_ref,
         o_ref, sto_ref, st_ref, kpad, vpad, bpad) = refs
    else:
        (q_ref, k_ref, v_ref, lf_ref, gate_ref, hn_ref,
         o_ref, sto_ref, st_ref, kpad, vpad, bpad) = refs
    i = pl.program_id(1)
    nc = tt // CHUNK

    @pl.when(i == 0)
    def _():
        if has_init:
            st_ref[...] = st0_ref[0]
        else:
            st_ref[...] = jnp.zeros(st_ref.shape, F32)

    q = q_ref[...]
    kk = k_ref[...]
    v = v_ref[...]
    row = lax.broadcasted_iota(jnp.int32, (tt, G), 0)
    r64 = row & (CHUNK - 1)
    rsub = row & (SUB - 1)
    lane_head = lax.broadcasted_iota(jnp.int32, (tt, G), 1) >> HD_SHIFT
    same_head = (lax.broadcasted_iota(jnp.int32, (G, G), 0) >> HD_SHIFT
                 == lax.broadcasted_iota(jnp.int32, (G, G), 1) >> HD_SHIFT)
    bd = jnp.where(same_head, 1.0, 0.0).astype(BF16)

    b = lf_ref[...]
    sh = 1
    while sh < CHUNK:
        b = b + jnp.where(r64 >= sh, pltpu.roll(b, sh, axis=0), 0.0)
        sh *= 2

    def chunk_row(r):
        return jnp.concatenate(
            [jnp.broadcast_to(b[c * CHUNK + r:c * CHUNK + r + 1, :], (CHUNK, G)) for c in range(nc)], axis=0)

    zpad = jnp.zeros((SUB, G), F32)
    kpad[0:SUB, :] = zpad
    vpad[0:SUB, :] = zpad
    bpad[0:SUB, :] = zpad
    kpad[SUB:SUB + tt, :] = kk
    vpad[SUB:SUB + tt, :] = v
    bpad[SUB:SUB + tt, :] = b
    o = jnp.zeros((tt, G), F32)
    for d in range(SUB):
        ks = kpad[SUB - d:SUB - d + tt, :]
        vs = vpad[SUB - d:SUB - d + tt, :]
        bs = bpad[SUB - d:SUB - d + tt, :]
        p = q * ks * jnp.exp(jnp.minimum(b - bs, 0.0))
        p = jnp.where(rsub >= d, p, 0.0)
        o = o + _dot(p.astype(BF16), bd) * vs

    nsub = CHUNK // SUB
    refs_b = [chunk_row(SUB * n - 1) for n in range(1, nsub)]
    sub = r64 >> SUB_SHIFT
    rq = b
    for n, rb in enumerate(refs_b, start=1):
        rq = jnp.where(sub == n, rb, rq)
    qt = q * jnp.exp(jnp.minimum(b - rq, 0.0))
    gr = SUB * nc
    lane_head_g = lax.broadcasted_iota(jnp.int32, (gr, G), 1) >> HD_SHIFT
    g_row = lax.broadcasted_iota(jnp.int32, (gr, tt), 0)
    g_col = lax.broadcasted_iota(jnp.int32, (gr, tt), 1)
    key_sub = jnp.where(g_row >> SUB_SHIFT == g_col >> CHUNK_SHIFT, (g_col & (CHUNK - 1)) >> SUB_SHIFT, nsub)
    key_sub4 = jnp.concatenate([key_sub] * H, axis=0)
    a_all = []
    for n, rb in enumerate(refs_b, start=1):
        qn = jnp.concatenate([qt[c * CHUNK + SUB * n:c * CHUNK + SUB * (n + 1)] for c in range(nc)], axis=0)
        q4n = _stack_heads(qn, lane_head_g).astype(BF16)
        kt = (kk * jnp.exp(jnp.minimum(rb - b, 0.0))).astype(BF16)
        a_all.append(jnp.where(key_sub4 < n, _dot_nt(q4n, kt), 0.0).astype(BF16))
    vb = v.astype(BF16)
    o_all = _dot(jnp.concatenate(a_all, axis=0), vb)
    o_sub = [_unstack_heads(o_all[(n - 1) * H * gr:n * H * gr], lane_head_g, gr) for n in range(1, nsub)]
    pieces = []
    for c in range(nc):
        pieces.append(jnp.zeros((SUB, G), F32))
        pieces += [o_sub[n - 1][c * SUB:(c + 1) * SUB] for n in range(1, nsub)]
    o = o + jnp.concatenate(pieces, axis=0)

    blast = chunk_row(CHUNK - 1)
    qs = (q * jnp.exp(b)).astype(BF16)
    kd = (kk * jnp.exp(jnp.minimum(blast - b, 0.0))).astype(BF16)
    dec = jnp.exp(blast)
    outs = []
    for c in range(nc):
        sl = slice(c * CHUNK, (c + 1) * CHUNK)
        st = st_ref[...]
        outs.append(_dot_nt(qs[sl], st.astype(BF16)))
        upd = _dot_tn(vb[sl], kd[sl])
        st_ref[...] = st * dec[c * CHUNK:c * CHUNK + 1, :] + jnp.where(same_head, upd, 0.0)
    o = o + jnp.concatenate(outs, axis=0)

    sq = o * o
    hi = sq.astype(BF16)
    lo = (sq - hi.astype(F32)).astype(BF16)
    ms = (_dot(hi, bd) + _dot(lo, bd)) * (1.0 / HD)
    gate = gate_ref[...]
    o_ref[...] = (o * lax.rsqrt(ms + EPS) * hn_ref[...] * (gate * _sigmoid(gate))).astype(BF16)

    @pl.when(i == pl.num_programs(1) - 1)
    def _():
        sto_ref[0] = st_ref[...]


def _hgrn(q, k, v, lf, gate, hn, st0, *, row_off, b, l, tt):
    nt = l // tt
    off = row_off // tt
    tok = pl.BlockSpec((tt, G), lambda bi, i: (off + bi * nt + i, 0))
    in_specs = [tok, tok, tok, tok, tok, pl.BlockSpec((1, G), lambda bi, i: (0, 0))]
    args = [q, k, v, lf, gate, hn]
    if st0 is not None:
        in_specs.append(pl.BlockSpec((1, G, G), lambda bi, i: (bi, 0, 0)))
        args.append(st0)
    return pl.pallas_call(
        functools.partial(_hgrn_kernel, tt=tt, has_init=st0 is not None),
        grid=(b, nt),
        in_specs=in_specs,
        out_specs=[pl.BlockSpec((tt, G), lambda bi, i: (bi * nt + i, 0)),
                   pl.BlockSpec((1, G, G), lambda bi, i: (bi, 0, 0))],
        out_shape=[jax.ShapeDtypeStruct((b * l, G), BF16), jax.ShapeDtypeStruct((b, G, G), F32)],
        scratch_shapes=[pltpu.VMEM((G, G), F32)] + [pltpu.VMEM((tt + SUB, G), F32)] * 3,
        compiler_params=_params("arbitrary", "arbitrary"),
        name="hgrn",
    )(*args)


def _c1_kernel(x_ref, m0_ref, m1_ref, m2_ref, m3_ref, wo_ref, g_ref, wq_ref, x1_ref, q_ref):
    acc = x_ref[...]
    for p, m_ref in enumerate((m0_ref, m1_ref, m2_ref, m3_ref)):
        acc = acc + _dot(m_ref[...], wo_ref[p * G:(p + 1) * G, :])
    x1_ref[...] = acc
    h = _rms(acc, g_ref[...]).astype(BF16)
    q_ref[...] = (_dot(h, wq_ref[...]) * (MEM_HD ** -0.5)).astype(BF16)


def _c1(x, mix, wo, g, wq, tt):
    t = x.shape[0]
    row = lambda i: (i, 0)
    fix = lambda i: (0, 0)
    return pl.pallas_call(
        _c1_kernel,
        grid=(t // tt,),
        in_specs=[pl.BlockSpec((tt, D_MODEL), row)] + [pl.BlockSpec((tt, G), row)] * 4
                 + [pl.BlockSpec((D_MODEL, D_MODEL), fix), pl.BlockSpec((1, D_MODEL), fix),
                    pl.BlockSpec((D_MODEL, D_MODEL), fix)],
        out_specs=[pl.BlockSpec((tt, D_MODEL), row), pl.BlockSpec((tt, D_MODEL), row)],
        out_shape=[jax.ShapeDtypeStruct((t, D_MODEL), F32), jax.ShapeDtypeStruct((t, D_MODEL), BF16)],
        compiler_params=_params("parallel"),
        name="outproj_memq",
    )(x, *mix, wo, g, wq)


def _memkv_kernel(m_ref, wk_ref, wv_ref, k_ref, kb_ref, v_ref, vb_ref):
    m = m_ref[...].astype(BF16)
    k = _dot(m, wk_ref[...])
    k_ref[...] = k
    kb_ref[...] = k.astype(BF16)
    v = _dot(m, wv_ref[...])
    v_ref[...] = v
    vb_ref[...] = v.astype(BF16)


def _memkv(mem, wk, wv, tt):
    t = mem.shape[0]
    row = lambda i: (i, 0)
    fix = lambda i: (0, 0)
    f32o = jax.ShapeDtypeStruct((t, D_MODEL), F32)
    b16o = jax.ShapeDtypeStruct((t, D_MODEL), BF16)
    return pl.pallas_call(
        _memkv_kernel,
        grid=(t // tt,),
        in_specs=[pl.BlockSpec((tt, D_MODEL), row), pl.BlockSpec((D_MODEL, D_MODEL), fix),
                  pl.BlockSpec((D_MODEL, D_MODEL), fix)],
        out_specs=[pl.BlockSpec((tt, D_MODEL), row)] * 4,
        out_shape=[f32o, b16o, f32o, b16o],
        compiler_params=_params("parallel"),
        name="memkv",
    )(mem, wk, wv)


def _memattn_kernel(q_ref, k_ref, v_ref, o_ref):
    outs = []
    for h in range(MEM_HEADS):
        sl = slice(h * MEM_HD, (h + 1) * MEM_HD)
        s = _dot_nt(q_ref[:, sl], k_ref[:, sl].astype(BF16))
        p = jnp.exp(s - jnp.max(s, axis=-1, keepdims=True))
        den = jnp.sum(p, axis=-1, keepdims=True)
        outs.append(_dot(p.astype(BF16), v_ref[:, sl].astype(BF16)) * (1.0 / den))
    o_ref[...] = jnp.concatenate(outs, axis=-1).astype(BF16)


def _memattn(q, k, v, *, row_off, b, l, tq):
    nq = l // tq
    off = row_off // tq
    return pl.pallas_call(
        _memattn_kernel,
        grid=(b, nq),
        in_specs=[pl.BlockSpec((tq, D_MODEL), lambda bi, i: (off + bi * nq + i, 0)),
                  pl.BlockSpec((N_MEM, D_MODEL), lambda bi, i: (bi, 0)),
                  pl.BlockSpec((N_MEM, D_MODEL), lambda bi, i: (bi, 0))],
        out_specs=pl.BlockSpec((tq, D_MODEL), lambda bi, i: (bi * nq + i, 0)),
        out_shape=jax.ShapeDtypeStruct((b * l, D_MODEL), BF16),
        compiler_params=_params("parallel", "parallel"),
        name="memattn",
    )(q, k, v)


def _c3_kernel(x1_ref, o_ref, wmo_ref, g_ref, wup_ref, wdn_ref, gf_ref, y_ref,
               x2_ref, h_ref, acc_ref, *, final):
    f = pl.program_id(1)

    @pl.when(f == 0)
    def _():
        x2 = x1_ref[...] + _dot(o_ref[...], wmo_ref[...])
        x2_ref[...] = x2
        h_ref[...] = _rms(x2, g_ref[...]).astype(BF16)
        acc_ref[...] = jnp.zeros(acc_ref.shape, F32)

    a = jnp.maximum(_dot(h_ref[...], wup_ref[...]), 0.0)
    acc_ref[...] += _dot((a * a).astype(BF16), wdn_ref[...])

    @pl.when(f == pl.num_programs(1) - 1)
    def _():
        x3 = x2_ref[...] + acc_ref[...]
        y_ref[...] = _rms(x3, gf_ref[...]) if final else x3


def _c3(x1, o, wmo, g, wup, wdn, gf, *, tt, tf, final):
    t = x1.shape[0]
    row = lambda i, f: (i, 0)
    fix = lambda i, f: (0, 0)
    return pl.pallas_call(
        functools.partial(_c3_kernel, final=final),
        grid=(t // tt, D_FF // tf),
        in_specs=[pl.BlockSpec((tt, D_MODEL), row), pl.BlockSpec((tt, D_MODEL), row),
                  pl.BlockSpec((D_MODEL, D_MODEL), fix), pl.BlockSpec((1, D_MODEL), fix),
                  pl.BlockSpec((D_MODEL, tf), lambda i, f: (0, f)),
                  pl.BlockSpec((tf, D_MODEL), lambda i, f: (f, 0)),
                  pl.BlockSpec((1, D_MODEL), fix)],
        out_specs=pl.BlockSpec((tt, D_MODEL), row),
        out_shape=jax.ShapeDtypeStruct((t, D_MODEL), F32),
        scratch_shapes=[pltpu.VMEM((tt, D_MODEL), F32), pltpu.VMEM((tt, D_MODEL), BF16),
                        pltpu.VMEM((tt, D_MODEL), F32)],
        compiler_params=_params("parallel", "arbitrary"),
        name="memout_mlp",
    )(x1, o, wmo, g, wup, wdn, gf)


def _block_diag_t(s):
    b = s.shape[0]
    eye = jnp.eye(H, dtype=s.dtype)
    st = jnp.swapaxes(s, 2, 3)
    return jnp.einsum('bhvk,hg->bhvgk', st, eye).reshape(b, G, G)


def _unblock_diag_t(st):
    b = st.shape[0]
    s5 = st.reshape(b, H, HD, H, HD)
    d = jnp.stack([s5[:, h, :, h, :] for h in range(H)], axis=1)
    return jnp.swapaxes(d, 2, 3)


def _row_cumsum(lf, tile, scale):
    b, s, _ = lf.shape
    sp = -(-s // tile) * tile
    x = jnp.swapaxes(lf, 1, 2).reshape(b * H, s)
    x = jnp.pad(x, ((0, 0), (0, sp - s)))
    return _cumsum_lanes(x, scale).reshape(b, H, sp)[:, :, :s]


def _tiles(c, n, t):
    b = c.shape[0]
    return jnp.swapaxes(c.reshape(b, H, n, t), 1, 2)


def kernel(x_prompt, x_sample, mem_prompt, cache_conv, cache_fox_k, cache_fox_v, cache_fox_logf,
           state_hgrn, cache_sb_k, cache_sb_v, cache_mem_k, cache_mem_v, norm_mix, w_in, b_fox_f,
           conv_w, conv_b, conv_ln_g, conv_ln_b, hgrn_lb, hgrn_norm, w_out, norm_mem, w_mq, w_mk,
           w_mv, w_mo, norm_ffn, w_up, w_down, norm_final):
    bp, sp, _ = x_prompt.shape
    bs, ls, _ = x_sample.shape
    depth = w_in.shape[0]
    past = cache_fox_k.shape[2]
    tp = bp * sp
    ts = bs * ls
    tok = 512
    tq_p = 512
    tt_h = 256
    tk_past = 512

    sm = jax.nn.softmax(hgrn_lb.astype(F32), axis=0)
    lower = jnp.clip(jnp.cumsum(sm, axis=0) - sm[0], 0.0, 1.0 - 1e-6)
    pos = lower > 0.0
    hconst = jnp.stack([jnp.log1p(-lower), jnp.log(jnp.where(pos, lower, 1.0)),
                        pos.astype(F32), 1.0 - lower], axis=1)

    o_ff = 5 * G
    w_in_r = jnp.concatenate(
        [w_in[:, :, :o_ff], w_in[:, :, o_ff + H:],
         jnp.pad(w_in[:, :, o_ff:o_ff + H], ((0, 0), (0, 0), (0, FF_PAD - H)))], axis=-1).astype(BF16)
    bff = jnp.pad(b_fox_f, ((0, 0), (0, FF_PAD - H)))[:, None, :]
    w_out_b = w_out.astype(BF16)
    w_mq_b = w_mq.astype(BF16)
    w_mk_b = w_mk.astype(BF16)
    w_mv_b = w_mv.astype(BF16)
    w_mo_b = w_mo.astype(BF16)
    w_up_b = w_up.astype(BF16)
    w_down_b = w_down.astype(BF16)
    hn = jnp.tile(hgrn_norm, (1, H))[:, None, :]
    conv_p3 = jnp.stack([conv_b, conv_ln_g, conv_ln_b], axis=1)

    def seq_last(c):
        return jnp.transpose(c, (0, 1, 3, 4, 2)).reshape(depth, bs, G, past)

    fkc, fvc, skc, svc = (seq_last(c) for c in (cache_fox_k, cache_fox_v, cache_sb_k, cache_sb_v))

    xp = x_prompt.reshape(tp, D_MODEL)
    xs = x_sample.reshape(ts, D_MODEL)
    mem2 = mem_prompt.reshape(bp * N_MEM, D_MODEL)
    zero_hist = jnp.zeros((bp, HALO, G), F32)
    bufs = None

    outs = {n: [] for n in ('conv_p', 'flf_p', 'hg_p', 'mk_p', 'mv_p',
                            'conv_s', 'fk_s', 'fv_s', 'flf_s', 'hg_s', 'sk_s', 'sv_s')}
    for l in range(depth):
        pp = _proj_seq(xp, norm_mix[l][None], w_in_r[l], bff[l], hconst[l], bufs,
                       layer=l, depth=depth, b=bp, l=sp, tt=tq_p)
        bufs = tuple(pp[n + '_t'] for n in _PROJ_KV)
        ps = _proj_rows(xs, norm_mix[l][None], w_in_r[l], bff[l], hconst[l], tok)

        conv_op = _conv(pp['u'], zero_hist, conv_w[l], conv_p3[l], row_off=0, b=bp, l=sp, tt=tok)
        conv_os = _conv(ps['u'], cache_conv[l], conv_w[l], conv_p3[l], row_off=0, b=bs, l=ls, tt=ls)

        flf_pt = pp['flf'][:, :H, :]
        flf_s = ps['flf'][:, :H].reshape(bs, ls, H)
        c_p = _cumsum_lanes(flf_pt.reshape(bp * H, sp), LOG2E).reshape(bp, H, sp)
        c_s = _row_cumsum(jnp.concatenate([cache_fox_logf[l].astype(F32), flf_s], axis=1), LANES, LOG2E)
        fox_op = _attention('fox', pp['fq'], pp['fk_tb'], pp['fv_tb'],
                            (_tiles(c_p, sp // tq_p, tq_p), jnp.swapaxes(c_p, 1, 2).reshape(tp, H)),
                            None, layer=l, b=bp, l=sp, tq=tq_p, tkp=tk_past)
        fox_os = _attention('fox', ps['fq'], ps['fk_b'], ps['fv_b'],
                            (_tiles(c_s[:, :, past:], 1, ls), jnp.swapaxes(c_s[:, :, past:], 1, 2).reshape(ts, H)),
                            (fkc, fvc, _tiles(c_s[:, :, :past], past // tk_past, tk_past)),
                            layer=l, b=bs, l=ls, tq=ls, tkp=tk_past)

        hg_op, st_p = _hgrn(pp['hq'], pp['hk'], pp['hv'], pp['hlf'], pp['hgate'], hn[l], None,
                            row_off=0, b=bp, l=sp, tt=tt_h)
        hg_os, st_s = _hgrn(ps['hq'], ps['hk'], ps['hv'], ps['hlf'], ps['hgate'], hn[l],
                            _block_diag_t(state_hgrn[l].astype(F32)), row_off=0, b=bs, l=ls, tt=ls)

        sb_op = _attention('sb', pp['sq'], pp['sk_tb'], pp['sv_tb'], (), None,
                           layer=l, b=bp, l=sp, tq=tq_p, tkp=tk_past)
        sb_os = _attention('sb', ps['sq'], ps['sk_b'], ps['sv_b'], (), (skc, svc),
                           layer=l, b=bs, l=ls, tq=ls, tkp=tk_past)

        x1p, qmp = _c1(xp, [conv_op, fox_op, hg_op, sb_op], w_out_b[l], norm_mem[l][None], w_mq_b[l], tok)
        x1s, qms = _c1(xs, [conv_os, fox_os, hg_os, sb_os], w_out_b[l], norm_mem[l][None], w_mq_b[l], tok)

        mk, mkb, mv, mvb = _memkv(mem2, w_mk_b[l], w_mv_b[l], N_MEM)
        om_p = _memattn(qmp, mkb, mvb, row_off=0, b=bp, l=sp, tq=tok)
        om_s = _memattn(qms, cache_mem_k[l].reshape(bs * N_MEM, D_MODEL),
                        cache_mem_v[l].reshape(bs * N_MEM, D_MODEL), row_off=0, b=bs, l=ls, tq=ls)

        final = l == depth - 1
        xp = _c3(x1p, om_p, w_mo_b[l], norm_ffn[l][None], w_up_b[l], w_down_b[l], norm_final[None],
                 tt=1024, tf=512, final=final)
        xs = _c3(x1s, om_s, w_mo_b[l], norm_ffn[l][None], w_up_b[l], w_down_b[l], norm_final[None],
                 tt=tok, tf=1024, final=final)

        u_s = jnp.concatenate([cache_conv[l].astype(F32), ps['u'].reshape(bs, ls, G)], axis=1)
        outs['conv_p'].append(pp['u'].reshape(bp, sp, G)[:, sp - HALO:])
        outs['conv_s'].append(u_s[:, ls:])
        for name in _PROJ_KV:
            outs[name + '_s'].append(ps[name].reshape(bs, ls, H, HD))
        outs['flf_p'].append(flf_pt)
        outs['flf_s'].append(flf_s)
        outs['hg_p'].append(_unblock_diag_t(st_p))
        outs['hg_s'].append(_unblock_diag_t(st_s))
        outs['mk_p'].append(mk.reshape(bp, N_MEM, MEM_HEADS, MEM_HD))
        outs['mv_p'].append(mv.reshape(bp, N_MEM, MEM_HEADS, MEM_HD))

    st = lambda n: jnp.stack(outs[n])
    kv_p = [jnp.transpose(t.reshape(depth, bp, H, HD, sp), (0, 1, 4, 2, 3)) for t in bufs]
    return (xp.reshape(bp, sp, D_MODEL), xs.reshape(bs, ls, D_MODEL),
            st('conv_p'), kv_p[0], kv_p[1], jnp.swapaxes(st('flf_p'), 2, 3), st('hg_p'),
            kv_p[2], kv_p[3], st('mk_p'), st('mv_p'),
            st('conv_s'), st('fk_s'), st('fv_s'), st('flf_s'), st('hg_s'),
            st('sk_s'), st('sv_s'))
```

```python
import functools

import jax
import jax.numpy as jnp
from jax import lax
from jax.experimental import pallas as pl
from jax.experimental.pallas import tpu as pltpu

F32 = jnp.float32
BF16 = jnp.bfloat16

D_MODEL = 1024
G = 256
H = 4
HD = 64
CONV_W = 31
HALO = CONV_W - 1
N_MEM = 256
MEM_HEADS = 4
MEM_HD = 256
D_FF = 4096
EPS = 1e-6
NEG_BIG = -1e30
SUB = 8
CHUNK = 64
SUB_SHIFT = SUB.bit_length() - 1
CHUNK_SHIFT = CHUNK.bit_length() - 1
HD_SHIFT = HD.bit_length() - 1
FF_PAD = 128
LANES = 128
SB_BLOCK = 256
ATTN_GROUPS = 8
ATTN_GROUP_MIN_ROWS = 256
LOG2E = 1.4426950408889634
SOFTPLUS_LINEAR = 64.0
QK_SCALE = HD ** -0.5 * LOG2E
VMEM_LIMIT_BYTES = 52 * 1024 * 1024


def _params(*sem):
    return pltpu.CompilerParams(dimension_semantics=sem, vmem_limit_bytes=VMEM_LIMIT_BYTES)


def _logsig(x):
    return jnp.minimum(x, 0.0) - jnp.log(1.0 + jnp.exp(-jnp.abs(x)))


def _sigmoid(x):
    return 1.0 / (1.0 + jnp.exp(-x))


def _rms(x, g):
    return x * lax.rsqrt(jnp.mean(x * x, axis=-1, keepdims=True) + EPS) * g


def _dot(a, b):
    return jnp.dot(a, b, preferred_element_type=F32)


def _dot_nt(a, b):
    return lax.dot_general(a, b, (((1,), (1,)), ((), ())), preferred_element_type=F32)


def _dot_tn(a, b):
    return lax.dot_general(a, b, (((0,), (0,)), ((), ())), preferred_element_type=F32)


def _stack_heads(x, lane_head):
    return jnp.concatenate([jnp.where(lane_head == h, x, jnp.zeros_like(x)) for h in range(H)], axis=0)


def _unstack_heads(x4, lane_head, t):
    out = jnp.zeros((t, G), x4.dtype)
    for h in range(H):
        out = jnp.where(lane_head == h, x4[h * t:(h + 1) * t], out)
    return out


def _proj_groups(x_ref, g_ref, w_ref, bff_ref, hc_ref):
    h = _rms(x_ref[...], g_ref[...]).astype(BF16)

    def col(i, n=G):
        return _dot(h, w_ref[:, i * G:i * G + n])

    yield 'u', col(0) * _sigmoid(col(1))
    yield 'fq', col(2) * QK_SCALE
    yield 'fk', col(3)
    yield 'fv', col(4)
    yield 'hq', col(5)
    hz = col(6)
    base = hc_ref[0:1, :] + _logsig(hz)
    c1 = hc_ref[1:2, :]
    lae = jnp.maximum(c1, base) + jnp.log(1.0 + jnp.exp(-jnp.abs(c1 - base)))
    yield 'hlf', jnp.where(hc_ref[2:3, :] > 0.5, lae, base)
    yield 'hk', hc_ref[3:4, :] * _sigmoid(-hz)
    yield 'hv', col(7)
    yield 'hgate', col(8)
    yield 'sq', col(9) * QK_SCALE
    yield 'sk', col(10)
    yield 'sv', col(11)
    yield 'flf', _logsig(col(12, FF_PAD) + bff_ref[...])


_PROJ_ROW_F32 = ('u', 'hq', 'hlf', 'hk', 'hv', 'hgate')
_PROJ_ROW_BF16 = ('fq', 'sq')
_PROJ_KV = ('fk', 'fv', 'sk', 'sv')


def _proj_rows_kernel(x_ref, g_ref, w_ref, bff_ref, hc_ref, *out_refs):
    names = _PROJ_ROW_F32 + _PROJ_ROW_BF16 + ('flf',) + tuple(n + s for n in _PROJ_KV for s in ('', '_b'))
    refs = dict(zip(names, out_refs))
    for name, val in _proj_groups(x_ref, g_ref, w_ref, bff_ref, hc_ref):
        if name in _PROJ_KV:
            refs[name][...] = val
            refs[name + '_b'][...] = val.astype(BF16)
        else:
            refs[name][...] = val.astype(refs[name].dtype)


def _proj_rows(x, g, w, bff, hc, tt):
    t = x.shape[0]
    row = lambda i: (i, 0)
    fix = lambda i: (0, 0)
    f32o = jax.ShapeDtypeStruct((t, G), F32)
    b16o = jax.ShapeDtypeStruct((t, G), BF16)
    outs = ([f32o] * len(_PROJ_ROW_F32) + [b16o] * len(_PROJ_ROW_BF16)
            + [jax.ShapeDtypeStruct((t, FF_PAD), F32)] + [f32o, b16o] * len(_PROJ_KV))
    names = _PROJ_ROW_F32 + _PROJ_ROW_BF16 + ('flf',) + tuple(n + s for n in _PROJ_KV for s in ('', '_b'))
    res = pl.pallas_call(
        _proj_rows_kernel,
        grid=(t // tt,),
        in_specs=[pl.BlockSpec((tt, D_MODEL), row), pl.BlockSpec((1, D_MODEL), fix),
                  pl.BlockSpec((D_MODEL, w.shape[1]), fix), pl.BlockSpec((1, FF_PAD), fix),
                  pl.BlockSpec((4, G), fix)],
        out_specs=[pl.BlockSpec((tt, o.shape[1]), row) for o in outs],
        out_shape=outs,
        compiler_params=_params("parallel"),
        name="proj_rows",
    )(x, g, w, bff, hc)
    return dict(zip(names, res))


def _proj_seq_kernel(x_ref, g_ref, w_ref, bff_ref, hc_ref, *rest, n_alias):
    out_refs = rest[n_alias:]
    names = (_PROJ_ROW_F32 + _PROJ_ROW_BF16 + ('flf',)
             + tuple(n + s for n in _PROJ_KV for s in ('_t', '_tb')))
    refs = dict(zip(names, out_refs))
    for name, val in _proj_groups(x_ref, g_ref, w_ref, bff_ref, hc_ref):
        if name in _PROJ_KV:
            vt = val.T
            refs[name + '_t'][0, 0] = vt
            refs[name + '_tb'][0, 0] = vt.astype(BF16)
        elif name == 'flf':
            refs[name][0] = val.T[0:8, :]
        else:
            refs[name][...] = val.astype(refs[name].dtype)


def _proj_seq(x, g, w, bff, hc, bufs, *, layer, depth, b, l, tt):
    t = x.shape[0]
    nt = l // tt
    row = lambda i: (i, 0)
    fix = lambda i: (0, 0)
    f32o = jax.ShapeDtypeStruct((t, G), F32)
    b16o = jax.ShapeDtypeStruct((t, G), BF16)
    stacked = jax.ShapeDtypeStruct((depth, b, G, l), F32)
    tiled = jax.ShapeDtypeStruct((b, nt, G, tt), BF16)
    outs = ([f32o] * len(_PROJ_ROW_F32) + [b16o] * len(_PROJ_ROW_BF16)
            + [jax.ShapeDtypeStruct((b, 8, l), F32)] + [stacked, tiled] * len(_PROJ_KV))
    ospecs = ([pl.BlockSpec((tt, G), row)] * (len(_PROJ_ROW_F32) + len(_PROJ_ROW_BF16))
              + [pl.BlockSpec((1, 8, tt), lambda i: (i // nt, 0, i % nt))]
              + [pl.BlockSpec((1, 1, G, tt), lambda i: (layer, i // nt, 0, i % nt)),
                 pl.BlockSpec((1, 1, G, tt), lambda i: (i // nt, i % nt, 0, 0))] * len(_PROJ_KV))
    names = (_PROJ_ROW_F32 + _PROJ_ROW_BF16 + ('flf',)
             + tuple(n + s for n in _PROJ_KV for s in ('_t', '_tb')))
    n_in = 5
    first_stacked = len(_PROJ_ROW_F32) + len(_PROJ_ROW_BF16) + 1
    alias_args = list(bufs)
    aliases = {n_in + k: first_stacked + 2 * k for k in range(len(_PROJ_KV))}
    res = pl.pallas_call(
        functools.partial(_proj_seq_kernel, n_alias=len(alias_args)),
        grid=(t // tt,),
        in_specs=[pl.BlockSpec((tt, D_MODEL), row), pl.BlockSpec((1, D_MODEL), fix),
                  pl.BlockSpec((D_MODEL, w.shape[1]), fix), pl.BlockSpec((1, FF_PAD), fix),
                  pl.BlockSpec((4, G), fix)] + [pl.BlockSpec(memory_space=pl.ANY)] * len(alias_args),
        out_specs=ospecs,
        out_shape=outs,
        input_output_aliases=aliases,
        compiler_params=_params("parallel"),
        name="proj_seq",
    )(x, g, w, bff, hc, *alias_args)
    return dict(zip(names, res))


def _cumsum_kernel(x_ref, o_ref, *, scale):
    x = x_ref[...]
    n = x.shape[1]
    lane = lax.broadcasted_iota(jnp.int32, x.shape, 1)
    sh = 1
    while sh < n:
        x = x + jnp.where(lane >= sh, pltpu.roll(x, sh, axis=1), 0.0)
        sh *= 2
    o_ref[...] = x * scale


def _cumsum_lanes(x, scale):
    return pl.pallas_call(
        functools.partial(_cumsum_kernel, scale=scale),
        out_shape=jax.ShapeDtypeStruct(x.shape, F32),
        compiler_params=pltpu.CompilerParams(vmem_limit_bytes=VMEM_LIMIT_BYTES),
        name="cumsum",
    )(x)


def _conv_kernel(u_ref, hist_ref, w_ref, p_ref, o_ref, ext_ref, *, tt):
    i = pl.program_id(1)
    base = 32 - HALO

    @pl.when(i == 0)
    def _():
        ext_ref[0:base, :] = jnp.zeros((base, G), F32)
        ext_ref[base:32, :] = hist_ref[0]
        ext_ref[32 + tt:40 + tt, :] = jnp.zeros((8, G), F32)

    ext_ref[32:32 + tt, :] = u_ref[...]
    acc = None
    for r in range(8):
        part = None
        for m in range((base + CONV_W - 1) // 8 + 1):
            j = 8 * m + r - base
            if 0 <= j < CONV_W:
                term = w_ref[j:j + 1, :] * ext_ref[8 * m:8 * m + tt + 8, :]
                part = term if part is None else part + term
        part = part[r:r + tt, :]
        acc = part if acc is None else acc + part
    yf = acc + p_ref[0:1, :]
    mu = jnp.mean(yf, axis=-1, keepdims=True)
    d = yf - mu
    var = jnp.mean(d * d, axis=-1, keepdims=True)
    yn = d * lax.rsqrt(var + EPS) * p_ref[1:2, :] + p_ref[2:3, :]
    o_ref[...] = (yn * _sigmoid(yn)).astype(BF16)
    ext_ref[base:32, :] = ext_ref[tt + base:tt + 32, :]


def _conv(u, hist, w, p, *, row_off, b, l, tt):
    nt = l // tt
    off = row_off // tt
    return pl.pallas_call(
        functools.partial(_conv_kernel, tt=tt),
        grid=(b, nt),
        in_specs=[pl.BlockSpec((tt, G), lambda bi, i: (off + bi * nt + i, 0)),
                  pl.BlockSpec((1, HALO, G), lambda bi, i: (bi, 0, 0)),
                  pl.BlockSpec((CONV_W, G), lambda bi, i: (0, 0)),
                  pl.BlockSpec((3, G), lambda bi, i: (0, 0))],
        out_specs=pl.BlockSpec((tt, G), lambda bi, i: (bi * nt + i, 0)),
        out_shape=jax.ShapeDtypeStruct((b * l, G), BF16),
        scratch_shapes=[pltpu.VMEM((tt + 40, G), F32)],
        compiler_params=_params("arbitrary", "arbitrary"),
        name="conv",
    )(u, hist, w, p)


def _lanes(x, w):
    if w < LANES:
        return x[:, :w]
    return x if w == LANES else jnp.concatenate([x] * (w // LANES), axis=1)


def _scores(q4, k, t_layout):
    return _dot(q4, k) if t_layout else _dot_nt(q4, k)


def _weighted(p4, v, t_layout):
    return _dot_nt(p4, v) if t_layout else _dot(p4, v)


def _fox_kernel(*refs, tq, tkp, n_past, cur_t):
    if n_past:
        (q_ref, kc_ref, vc_ref, ckc_ref, cq_ref, kp_ref, vp_ref, ckp_ref,
         o_ref, m_ref, l_ref, acc_ref) = refs
    else:
        q_ref, kc_ref, vc_ref, ckc_ref, cq_ref, o_ref, m_ref, l_ref, acc_ref = refs
    i = pl.program_id(1)
    lane_head = lax.broadcasted_iota(jnp.int32, (tq, G), 1) >> HD_SHIFT
    q4 = _stack_heads(q_ref[...], lane_head)
    cq = cq_ref[...]
    cqb = [jnp.broadcast_to(cq[:, h:h + 1], (tq, LANES)) for h in range(H)]
    tri = (lax.broadcasted_iota(jnp.int32, (tq, tq), 1)
           <= lax.broadcasted_iota(jnp.int32, (tq, tq), 0))
    m_ref[...] = jnp.full(m_ref.shape, NEG_BIG, F32)
    l_ref[...] = jnp.zeros(l_ref.shape, F32)
    acc_ref[...] = jnp.zeros(acc_ref.shape, F32)

    def step(k, v, ck, masked, t_layout):
        s = _scores(q4, k, t_layout)
        w = s.shape[1]
        ps = []
        for h in range(H):
            sl = slice(h * tq, (h + 1) * tq)
            sh = s[sl] - ck[h:h + 1, :]
            if masked:
                sh = jnp.where(tri, sh, NEG_BIG)
            m_old = m_ref[sl]
            m_new = jnp.maximum(m_old, jnp.max(sh, axis=-1, keepdims=True) + cqb[h])
            p = jnp.exp2(sh - _lanes(m_new - cqb[h], w))
            alpha = jnp.exp2(m_old - m_new)
            l_ref[sl] = alpha * l_ref[sl] + jnp.sum(p, axis=-1, keepdims=True)
            m_ref[sl] = m_new
            acc_ref[sl] = acc_ref[sl] * _lanes(alpha, G)
            ps.append(p.astype(BF16))
        acc_ref[...] += _weighted(jnp.concatenate(ps, axis=0), v, t_layout)

    for j in range(n_past):
        sl = slice(j * tkp, (j + 1) * tkp)
        step(kp_ref[0, 0, :, sl].astype(BF16), vp_ref[0, 0, :, sl].astype(BF16), ckp_ref[0, j], False, True)

    if cur_t:
        def cur_body(j, c):
            step(kc_ref[0, j], vc_ref[0, j], ckc_ref[0, j], False, True)
            return c
        lax.fori_loop(0, i, cur_body, 0)
        step(kc_ref[0, i], vc_ref[0, i], ckc_ref[0, i], True, True)
    else:
        step(kc_ref[...], vc_ref[...], ckc_ref[0, 0], True, False)

    acc4 = jnp.concatenate(
        [acc_ref[h * tq:(h + 1) * tq] * _lanes(1.0 / l_ref[h * tq:(h + 1) * tq], G) for h in range(H)],
        axis=0)
    o_ref[...] = _unstack_heads(acc4, lane_head, tq).astype(BF16)


def _sb_kernel(*refs, tq, tkp, n_past, cur_t):
    if n_past:
        q_ref, kc_ref, vc_ref, kp_ref, vp_ref, o_ref, r_ref, acc_ref = refs
    else:
        q_ref, kc_ref, vc_ref, o_ref, r_ref, acc_ref = refs
    i = pl.program_id(1)
    lane_head = lax.broadcasted_iota(jnp.int32, (tq, G), 1) >> HD_SHIFT
    q4 = _stack_heads(q_ref[...], lane_head)
    r_ref[...] = jnp.zeros(r_ref.shape, F32)
    acc_ref[...] = jnp.zeros(acc_ref.shape, F32)

    def later_matrix(n):
        later = (lax.broadcasted_iota(jnp.int32, (n, n), 0) > lax.broadcasted_iota(jnp.int32, (n, n), 1))
        return jnp.where(later, 1.0, 0.0).astype(BF16)

    widths = {min(tq, SB_BLOCK)} | ({min(tkp, SB_BLOCK)} if n_past else set())
    later = {n: later_matrix(n) for n in widths}

    tri = (lax.broadcasted_iota(jnp.int32, (tq, tq), 1)
           < lax.broadcasted_iota(jnp.int32, (tq, tq), 0))
    valid = jnp.concatenate([tri] * H, axis=0)
    groups = ATTN_GROUPS if H * tq // ATTN_GROUPS >= ATTN_GROUP_MIN_ROWS else 1

    def step(k, v, masked, t_layout):
        for g in range(groups):
            rows = slice(g * (H * tq // groups), (g + 1) * (H * tq // groups))
            z = _scores(q4[rows], k, t_layout)
            w = z.shape[1]
            sp = jnp.where(z > SOFTPLUS_LINEAR, z, jnp.log2(1.0 + jnp.exp2(z)))
            u = jnp.where(valid[rows], sp, 0.0) if masked else sp
            bw = min(w, SB_BLOCK)
            r_run = r_ref[rows]
            rests = [None] * (w // bw)
            for blk in reversed(range(w // bw)):
                ub = u[:, blk * bw:(blk + 1) * bw]
                rests[blk] = _dot(ub.astype(BF16), later[bw]) + _lanes(r_run, bw)
                r_run = r_run + jnp.sum(ub, axis=-1, keepdims=True)
            rest = rests[0] if len(rests) == 1 else jnp.concatenate(rests, axis=1)
            a = jnp.exp2(z - sp - rest)
            if masked:
                a = jnp.where(valid[rows], a, 0.0)
            acc_ref[rows] += _weighted(a.astype(BF16), v, t_layout)
            r_ref[rows] = r_run

    if cur_t:
        step(kc_ref[0, i], vc_ref[0, i], True, True)

        def cur_body(n, c):
            j = i - 1 - n
            step(kc_ref[0, j], vc_ref[0, j], False, True)
            return c
        lax.fori_loop(0, i, cur_body, 0)
    else:
        step(kc_ref[...], vc_ref[...], True, False)

    for j in reversed(range(n_past)):
        sl = slice(j * tkp, (j + 1) * tkp)
        step(kp_ref[0, 0, :, sl].astype(BF16), vp_ref[0, 0, :, sl].astype(BF16), False, True)

    o_ref[...] = _unstack_heads(acc_ref[...], lane_head, tq).astype(BF16)


def _attention(kind, q, kc, vc, extra, past, *, layer, b, l, tq, tkp):
    nq = l // tq
    cur_t = kc.ndim == 4
    assert cur_t or nq == 1
    if cur_t:
        kv_spec = pl.BlockSpec((1, nq, G, tq), lambda bi, i: (bi, 0, 0, 0))
    else:
        kv_spec = pl.BlockSpec((l, G), lambda bi, i: (bi, 0))
    in_specs = [pl.BlockSpec((tq, G), lambda bi, i: (bi * nq + i, 0)), kv_spec, kv_spec]
    args = [q, kc, vc]
    if kind == 'fox':
        ckc, cq = extra
        in_specs += [pl.BlockSpec((1, nq, H, tq), lambda bi, i: (bi, 0, 0, 0)),
                     pl.BlockSpec((tq, H), lambda bi, i: (bi * nq + i, 0))]
        args += [ckc, cq]
    n_past = 0
    if past is not None:
        plen = past[0].shape[3]
        n_past = plen // tkp
        cache_spec = pl.BlockSpec((1, 1, G, plen), lambda bi, i: (layer, bi, 0, 0))
        in_specs += [cache_spec, cache_spec]
        args += [past[0], past[1]]
        if kind == 'fox':
            in_specs.append(pl.BlockSpec((1, n_past, H, tkp), lambda bi, i: (bi, 0, 0, 0)))
            args.append(past[2])
    stat = pltpu.VMEM((H * tq, LANES), F32)
    acc = pltpu.VMEM((H * tq, G), F32)
    body = _fox_kernel if kind == 'fox' else _sb_kernel
    return pl.pallas_call(
        functools.partial(body, tq=tq, tkp=tkp, n_past=n_past, cur_t=cur_t),
        grid=(b, nq),
        in_specs=in_specs,
        out_specs=pl.BlockSpec((tq, G), lambda bi, i: (bi * nq + i, 0)),
        out_shape=jax.ShapeDtypeStruct((b * l, G), BF16),
        scratch_shapes=[stat, stat, acc] if kind == 'fox' else [stat, acc],
        compiler_params=_params("parallel", "arbitrary"),
        name=kind,
    )(*args)


def _hgrn_kernel(*refs, tt, has_init):
    if has_init:
        (q_ref, k_ref, v_ref, lf_ref, gate_ref, hn_ref, st0_ref,
         o_ref, sto_ref, st_ref, kpad, vpad, bpad) = refs
    else:
        (q_ref, k_ref, v_ref, lf_ref, gate_ref, hn_ref,
         o_ref, sto_ref, st_ref, kpad, vpad, bpad) = refs
    i = pl.program_id(1)
    nc = tt // CHUNK

    @pl.when(i == 0)
    def _():
        if has_init:
            st_ref[...] = st0_ref[0]
        else:
            st_ref[...] = jnp.zeros(st_ref.shape, F32)

    q = q_ref[...]
    kk = k_ref[...]
    v = v_ref[...]
    row = lax.broadcasted_iota(jnp.int32, (tt, G), 0)
    r64 = row & (CHUNK - 1)
    rsub = row & (SUB - 1)
    lane_head = lax.broadcasted_iota(jnp.int32, (tt, G), 1) >> HD_SHIFT
    same_head = (lax.broadcasted_iota(jnp.int32, (G, G), 0) >> HD_SHIFT
                 == lax.broadcasted_iota(jnp.int32, (G, G), 1) >> HD_SHIFT)
    bd = jnp.where(same_head, 1.0, 0.0).astype(BF16)

    b = lf_ref[...]
    sh = 1
    while sh < CHUNK:
        b = b + jnp.where(r64 >= sh, pltpu.roll(b, sh, axis=0), 0.0)
        sh *= 2

    def chunk_row(r):
        return jnp.concatenate(
            [jnp.broadcast_to(b[c * CHUNK + r:c * CHUNK + r + 1, :], (CHUNK, G)) for c in range(nc)], axis=0)

    zpad = jnp.zeros((SUB, G), F32)
    kpad[0:SUB, :] = zpad
    vpad[0:SUB, :] = zpad
    bpad[0:SUB, :] = zpad
    kpad[SUB:SUB + tt, :] = kk
    vpad[SUB:SUB + tt, :] = v
    bpad[SUB:SUB + tt, :] = b
    o = jnp.zeros((tt, G), F32)
    for d in range(SUB):
        ks = kpad[SUB - d:SUB - d + tt, :]
        vs = vpad[SUB - d:SUB - d + tt, :]
        bs = bpad[SUB - d:SUB - d + tt, :]
        p = q * ks * jnp.exp(jnp.minimum(b - bs, 0.0))
        p = jnp.where(rsub >= d, p, 0.0)
        o = o + _dot(p.astype(BF16), bd) * vs

    nsub = CHUNK // SUB
    refs_b = [chunk_row(SUB * n - 1) for n in range(1, nsub)]
    sub = r64 >> SUB_SHIFT
    rq = b
    for n, rb in enumerate(refs_b, start=1):
        rq = jnp.where(sub == n, rb, rq)
    qt = q * jnp.exp(jnp.minimum(b - rq, 0.0))
    gr = SUB * nc
    lane_head_g = lax.broadcasted_iota(jnp.int32, (gr, G), 1) >> HD_SHIFT
    g_row = lax.broadcasted_iota(jnp.int32, (gr, tt), 0)
    g_col = lax.broadcasted_iota(jnp.int32, (gr, tt), 1)
    key_sub = jnp.where(g_row >> SUB_SHIFT == g_col >> CHUNK_SHIFT, (g_col & (CHUNK - 1)) >> SUB_SHIFT, nsub)
    key_sub4 = jnp.concatenate([key_sub] * H, axis=0)
    a_all = []
    for n, rb in enumerate(refs_b, start=1):
        qn = jnp.concatenate([qt[c * CHUNK + SUB * n:c * CHUNK + SUB * (n + 1)] for c in range(nc)], axis=0)
        q4n = _stack_heads(qn, lane_head_g).astype(BF16)
        kt = (kk * jnp.exp(jnp.minimum(rb - b, 0.0))).astype(BF16)
        a_all.append(jnp.where(key_sub4 < n, _dot_nt(q4n, kt), 0.0).astype(BF16))
    vb = v.astype(BF16)
    o_all = _dot(jnp.concatenate(a_all, axis=0), vb)
    o_sub = [_unstack_heads(o_all[(n - 1) * H * gr:n * H * gr], lane_head_g, gr) for n in range(1, nsub)]
    pieces = []
    for c in range(nc):
        pieces.append(jnp.zeros((SUB, G), F32))
        pieces += [o_sub[n - 1][c * SUB:(c + 1) * SUB] for n in range(1, nsub)]
    o = o + jnp.concatenate(pieces, axis=0)

    blast = chunk_row(CHUNK - 1)
    qs = (q * jnp.exp(b)).astype(BF16)
    kd = (kk * jnp.exp(jnp.minimum(blast - b, 0.0))).astype(BF16)
    dec = jnp.exp(blast)
    outs = []
    for c in range(nc):
        sl = slice(c * CHUNK, (c + 1) * CHUNK)
        st = st_ref[...]
        outs.append(_dot_nt(qs[sl], st.astype(BF16)))
        upd = _dot_tn(vb[sl], kd[sl])
        st_ref[...] = st * dec[c * CHUNK:c * CHUNK + 1, :] + jnp.where(same_head, upd, 0.0)
    o = o + jnp.concatenate(outs, axis=0)

    sq = o * o
    hi = sq.astype(BF16)
    lo = (sq - hi.astype(F32)).astype(BF16)
    ms = (_dot(hi, bd) + _dot(lo, bd)) * (1.0 / HD)
    gate = gate_ref[...]
    o_ref[...] = (o * lax.rsqrt(ms + EPS) * hn_ref[...] * (gate * _sigmoid(gate))).astype(BF16)

    @pl.when(i == pl.num_programs(1) - 1)
    def _():
        sto_ref[0] = st_ref[...]


def _hgrn(q, k, v, lf, gate, hn, st0, *, row_off, b, l, tt):
    nt = l // tt
    off = row_off // tt
    tok = pl.BlockSpec((tt, G), lambda bi, i: (off + bi * nt + i, 0))
    in_specs = [tok, tok, tok, tok, tok, pl.BlockSpec((1, G), lambda bi, i: (0, 0))]
    args = [q, k, v, lf, gate, hn]
    if st0 is not None:
        in_specs.append(pl.BlockSpec((1, G, G), lambda bi, i: (bi, 0, 0)))
        args.append(st0)
    return pl.pallas_call(
        functools.partial(_hgrn_kernel, tt=tt, has_init=st0 is not None),
        grid=(b, nt),
        in_specs=in_specs,
        out_specs=[pl.BlockSpec((tt, G), lambda bi, i: (bi * nt + i, 0)),
                   pl.BlockSpec((1, G, G), lambda bi, i: (bi, 0, 0))],
        out_shape=[jax.ShapeDtypeStruct((b * l, G), BF16), jax.ShapeDtypeStruct((b, G, G), F32)],
        scratch_shapes=[pltpu.VMEM((G, G), F32)] + [pltpu.VMEM((tt + SUB, G), F32)] * 3,
        compiler_params=_params("arbitrary", "arbitrary"),
        name="hgrn",
    )(*args)


def _c1_kernel(x_ref, m0_ref, m1_ref, m2_ref, m3_ref, wo_ref, g_ref, wq_ref, x1_ref, q_ref):
    acc = x_ref[...]
    for p, m_ref in enumerate((m0_ref, m1_ref, m2_ref, m3_ref)):
        acc = acc + _dot(m_ref[...], wo_ref[p * G:(p + 1) * G, :])
    x1_ref[...] = acc
    h = _rms(acc, g_ref[...]).astype(BF16)
    q_ref[...] = (_dot(h, wq_ref[...]) * (MEM_HD ** -0.5)).astype(BF16)


def _c1(x, mix, wo, g, wq, tt):
    t = x.shape[0]
    row = lambda i: (i, 0)
    fix = lambda i: (0, 0)
    return pl.pallas_call(
        _c1_kernel,
        grid=(t // tt,),
        in_specs=[pl.BlockSpec((tt, D_MODEL), row)] + [pl.BlockSpec((tt, G), row)] * 4
                 + [pl.BlockSpec((D_MODEL, D_MODEL), fix), pl.BlockSpec((1, D_MODEL), fix),
                    pl.BlockSpec((D_MODEL, D_MODEL), fix)],
        out_specs=[pl.BlockSpec((tt, D_MODEL), row), pl.BlockSpec((tt, D_MODEL), row)],
        out_shape=[jax.ShapeDtypeStruct((t, D_MODEL), F32), jax.ShapeDtypeStruct((t, D_MODEL), BF16)],
        compiler_params=_params("parallel"),
        name="outproj_memq",
    )(x, *mix, wo, g, wq)


def _memkv_kernel(m_ref, wk_ref, wv_ref, k_ref, kb_ref, v_ref, vb_ref):
    m = m_ref[...].astype(BF16)
    k = _dot(m, wk_ref[...])
    k_ref[...] = k
    kb_ref[...] = k.astype(BF16)
    v = _dot(m, wv_ref[...])
    v_ref[...] = v
    vb_ref[...] = v.astype(BF16)


def _memkv(mem, wk, wv, tt):
    t = mem.shape[0]
    row = lambda i: (i, 0)
    fix = lambda i: (0, 0)
    f32o = jax.ShapeDtypeStruct((t, D_MODEL), F32)
    b16o = jax.ShapeDtypeStruct((t, D_MODEL), BF16)
    return pl.pallas_call(
        _memkv_kernel,
        grid=(t // tt,),
        in_specs=[pl.BlockSpec((tt, D_MODEL), row), pl.BlockSpec((D_MODEL, D_MODEL), fix),
                  pl.BlockSpec((D_MODEL, D_MODEL), fix)],
        out_specs=[pl.BlockSpec((tt, D_MODEL), row)] * 4,
        out_shape=[f32o, b16o, f32o, b16o],
        compiler_params=_params("parallel"),
        name="memkv",
    )(mem, wk, wv)


def _memattn_kernel(q_ref, k_ref, v_ref, o_ref):
    outs = []
    for h in range(MEM_HEADS):
        sl = slice(h * MEM_HD, (h + 1) * MEM_HD)
        s = _dot_nt(q_ref[:, sl], k_ref[:, sl].astype(BF16))
        p = jnp.exp(s - jnp.max(s, axis=-1, keepdims=True))
        den = jnp.sum(p, axis=-1, keepdims=True)
        outs.append(_dot(p.astype(BF16), v_ref[:, sl].astype(BF16)) * (1.0 / den))
    o_ref[...] = jnp.concatenate(outs, axis=-1).astype(BF16)


def _memattn(q, k, v, *, row_off, b, l, tq):
    nq = l // tq
    off = row_off // tq
    return pl.pallas_call(
        _memattn_kernel,
        grid=(b, nq),
        in_specs=[pl.BlockSpec((tq, D_MODEL), lambda bi, i: (off + bi * nq + i, 0)),
                  pl.BlockSpec((N_MEM, D_MODEL), lambda bi, i: (bi, 0)),
                  pl.BlockSpec((N_MEM, D_MODEL), lambda bi, i: (bi, 0))],
        out_specs=pl.BlockSpec((tq, D_MODEL), lambda bi, i: (bi * nq + i, 0)),
        out_shape=jax.ShapeDtypeStruct((b * l, D_MODEL), BF16),
        compiler_params=_params("parallel", "parallel"),
        name="memattn",
    )(q, k, v)


def _c3_kernel(x1_ref, o_ref, wmo_ref, g_ref, wup_ref, wdn_ref, gf_ref, y_ref, *, final, tf):
    x2 = x1_ref[...] + _dot(o_ref[...], wmo_ref[...])
    h = _rms(x2, g_ref[...]).astype(BF16)
    x3 = x2
    for f in range(D_FF // tf):
        a = jnp.maximum(_dot(h, wup_ref[:, f * tf:(f + 1) * tf]), 0.0)
        x3 = x3 + _dot((a * a).astype(BF16), wdn_ref[f * tf:(f + 1) * tf, :])
    y_ref[...] = _rms(x3, gf_ref[...]) if final else x3


def _c3(x1, o, wmo, g, wup, wdn, gf, *, tt, tf, final):
    t = x1.shape[0]
    row = lambda i: (i, 0)
    fix = lambda i: (0, 0)
    resident = lambda shape: pl.BlockSpec(shape, fix, pipeline_mode=pl.Buffered(1))
    return pl.pallas_call(
        functools.partial(_c3_kernel, final=final, tf=tf),
        grid=(t // tt,),
        in_specs=[pl.BlockSpec((tt, D_MODEL), row), pl.BlockSpec((tt, D_MODEL), row),
                  resident((D_MODEL, D_MODEL)), pl.BlockSpec((1, D_MODEL), fix),
                  resident((D_MODEL, D_FF)), resident((D_FF, D_MODEL)),
                  pl.BlockSpec((1, D_MODEL), fix)],
        out_specs=pl.BlockSpec((tt, D_MODEL), row),
        out_shape=jax.ShapeDtypeStruct((t, D_MODEL), F32),
        compiler_params=_params("parallel"),
        name="memout_mlp",
    )(x1, o, wmo, g, wup, wdn, gf)


def _block_diag_t(s):
    b = s.shape[0]
    eye = jnp.eye(H, dtype=s.dtype)
    st = jnp.swapaxes(s, 2, 3)
    return jnp.einsum('bhvk,hg->bhvgk', st, eye).reshape(b, G, G)


def _unblock_diag_t(st):
    b = st.shape[0]
    s5 = st.reshape(b, H, HD, H, HD)
    d = jnp.stack([s5[:, h, :, h, :] for h in range(H)], axis=1)
    return jnp.swapaxes(d, 2, 3)


def _row_cumsum(lf, tile, scale):
    b, s, _ = lf.shape
    sp = -(-s // tile) * tile
    x = jnp.swapaxes(lf, 1, 2).reshape(b * H, s)
    x = jnp.pad(x, ((0, 0), (0, sp - s)))
    return _cumsum_lanes(x, scale).reshape(b, H, sp)[:, :, :s]


def _tiles(c, n, t):
    b = c.shape[0]
    return jnp.swapaxes(c.reshape(b, H, n, t), 1, 2)


def kernel(x_prompt, x_sample, mem_prompt, cache_conv, cache_fox_k, cache_fox_v, cache_fox_logf,
           state_hgrn, cache_sb_k, cache_sb_v, cache_mem_k, cache_mem_v, norm_mix, w_in, b_fox_f,
           conv_w, conv_b, conv_ln_g, conv_ln_b, hgrn_lb, hgrn_norm, w_out, norm_mem, w_mq, w_mk,
           w_mv, w_mo, norm_ffn, w_up, w_down, norm_final):
    bp, sp, _ = x_prompt.shape
    bs, ls, _ = x_sample.shape
    depth = w_in.shape[0]
    past = cache_fox_k.shape[2]
    tp = bp * sp
    ts = bs * ls
    tok = 512
    tq_p = 512
    tt_h = 512
    tk_past = 512

    sm = jax.nn.softmax(hgrn_lb.astype(F32), axis=0)
    lower = jnp.clip(jnp.cumsum(sm, axis=0) - sm[0], 0.0, 1.0 - 1e-6)
    pos = lower > 0.0
    hconst = jnp.stack([jnp.log1p(-lower), jnp.log(jnp.where(pos, lower, 1.0)),
                        pos.astype(F32), 1.0 - lower], axis=1)

    o_ff = 5 * G
    w_in_r = jnp.concatenate(
        [w_in[:, :, :o_ff], w_in[:, :, o_ff + H:],
         jnp.pad(w_in[:, :, o_ff:o_ff + H], ((0, 0), (0, 0), (0, FF_PAD - H)))], axis=-1).astype(BF16)
    bff = jnp.pad(b_fox_f, ((0, 0), (0, FF_PAD - H)))[:, None, :]
    w_out_b = w_out.astype(BF16)
    w_mq_b = w_mq.astype(BF16)
    w_mk_b = w_mk.astype(BF16)
    w_mv_b = w_mv.astype(BF16)
    w_mo_b = w_mo.astype(BF16)
    w_up_b = w_up.astype(BF16)
    w_down_b = w_down.astype(BF16)
    hn = jnp.tile(hgrn_norm, (1, H))[:, None, :]
    conv_p3 = jnp.stack([conv_b, conv_ln_g, conv_ln_b], axis=1)

    def seq_last(c):
        return jnp.transpose(c, (0, 1, 3, 4, 2)).reshape(depth, bs, G, past)

    fkc, fvc, skc, svc = (seq_last(c) for c in (cache_fox_k, cache_fox_v, cache_sb_k, cache_sb_v))

    st0_all = _block_diag_t(state_hgrn.astype(F32).reshape(depth * bs, H, HD, HD)).reshape(depth, bs, G, G)
    xp = x_prompt.reshape(tp, D_MODEL)
    xs = x_sample.reshape(ts, D_MODEL)
    mem2 = mem_prompt.reshape(bp * N_MEM, D_MODEL)
    zero_hist = jnp.zeros((bp, HALO, G), F32)
    bufs = tuple(jnp.zeros((depth, bp, G, sp), F32) for _ in _PROJ_KV)

    outs = {n: [] for n in ('conv_p', 'flf_p', 'hg_p', 'mk_p', 'mv_p',
                            'conv_s', 'fk_s', 'fv_s', 'flf_s', 'hg_s', 'sk_s', 'sv_s')}
    for l in range(depth):
        pp = _proj_seq(xp, norm_mix[l][None], w_in_r[l], bff[l], hconst[l], bufs,
                       layer=l, depth=depth, b=bp, l=sp, tt=tq_p)
        bufs = tuple(pp[n + '_t'] for n in _PROJ_KV)
        ps = _proj_rows(xs, norm_mix[l][None], w_in_r[l], bff[l], hconst[l], tok)

        conv_op = _conv(pp['u'], zero_hist, conv_w[l], conv_p3[l], row_off=0, b=bp, l=sp, tt=tok)
        conv_os = _conv(ps['u'], cache_conv[l], conv_w[l], conv_p3[l], row_off=0, b=bs, l=ls, tt=ls)

        flf_pt = pp['flf'][:, :H, :]
        flf_s = ps['flf'][:, :H].reshape(bs, ls, H)
        c_p = _cumsum_lanes(flf_pt.reshape(bp * H, sp), LOG2E).reshape(bp, H, sp)
        c_s = _row_cumsum(jnp.concatenate([cache_fox_logf[l].astype(F32), flf_s], axis=1), LANES, LOG2E)
        fox_op = _attention('fox', pp['fq'], pp['fk_tb'], pp['fv_tb'],
                            (_tiles(c_p, sp // tq_p, tq_p), jnp.swapaxes(c_p, 1, 2).reshape(tp, H)),
                            None, layer=l, b=bp, l=sp, tq=tq_p, tkp=tk_past)
        fox_os = _attention('fox', ps['fq'], ps['fk_b'], ps['fv_b'],
                            (_tiles(c_s[:, :, past:], 1, ls), jnp.swapaxes(c_s[:, :, past:], 1, 2).reshape(ts, H)),
                            (fkc, fvc, _tiles(c_s[:, :, :past], past // tk_past, tk_past)),
                            layer=l, b=bs, l=ls, tq=ls, tkp=tk_past)

        hg_op, st_p = _hgrn(pp['hq'], pp['hk'], pp['hv'], pp['hlf'], pp['hgate'], hn[l], None,
                            row_off=0, b=bp, l=sp, tt=tt_h)
        hg_os, st_s = _hgrn(ps['hq'], ps['hk'], ps['hv'], ps['hlf'], ps['hgate'], hn[l],
                            st0_all[l], row_off=0, b=bs, l=ls, tt=ls)

        sb_op = _attention('sb', pp['sq'], pp['sk_tb'], pp['sv_tb'], (), None,
                           layer=l, b=bp, l=sp, tq=tq_p, tkp=tk_past)
        sb_os = _attention('sb', ps['sq'], ps['sk_b'], ps['sv_b'], (), (skc, svc),
                           layer=l, b=bs, l=ls, tq=ls, tkp=tk_past)

        x1p, qmp = _c1(xp, [conv_op, fox_op, hg_op, sb_op], w_out_b[l], norm_mem[l][None], w_mq_b[l], tok)
        x1s, qms = _c1(xs, [conv_os, fox_os, hg_os, sb_os], w_out_b[l], norm_mem[l][None], w_mq_b[l], tok)

        mk, mkb, mv, mvb = _memkv(mem2, w_mk_b[l], w_mv_b[l], N_MEM)
        om_p = _memattn(qmp, mkb, mvb, row_off=0, b=bp, l=sp, tq=tok)
        om_s = _memattn(qms, cache_mem_k[l].reshape(bs * N_MEM, D_MODEL),
                        cache_mem_v[l].reshape(bs * N_MEM, D_MODEL), row_off=0, b=bs, l=ls, tq=ls)

        final = l == depth - 1
        xp = _c3(x1p, om_p, w_mo_b[l], norm_ffn[l][None], w_up_b[l], w_down_b[l], norm_final[None],
                 tt=tok, tf=1024, final=final)
        xs = _c3(x1s, om_s, w_mo_b[l], norm_ffn[l][None], w_up_b[l], w_down_b[l], norm_final[None],
                 tt=tok, tf=1024, final=final)

        u_s = jnp.concatenate([cache_conv[l].astype(F32), ps['u'].reshape(bs, ls, G)], axis=1)
        outs['conv_p'].append(pp['u'].reshape(bp, sp, G)[:, sp - HALO:])
        outs['conv_s'].append(u_s[:, ls:])
        for name in _PROJ_KV:
            outs[name + '_s'].append(ps[name])
        outs['flf_p'].append(flf_pt)
        outs['flf_s'].append(flf_s)
        outs['hg_p'].append(st_p)
        outs['hg_s'].append(st_s)
        outs['mk_p'].append(mk)
        outs['mv_p'].append(mv)

    st = lambda n: jnp.stack(outs[n])
    kv_p = [jnp.transpose(t.reshape(depth, bp, H, HD, sp), (0, 1, 4, 2, 3)) for t in bufs]
    kv_s = {n: st(n + '_s').reshape(depth, bs, ls, H, HD) for n in _PROJ_KV}
    mem_p = [st(n).reshape(depth, bp, N_MEM, MEM_HEADS, MEM_HD) for n in ('mk_p', 'mv_p')]
    hg_p = _unblock_diag_t(st('hg_p').reshape(depth * bp, G, G)).reshape(depth, bp, H, HD, HD)
    hg_s = _unblock_diag_t(st('hg_s').reshape(depth * bs, G, G)).reshape(depth, bs, H, HD, HD)
    return (xp.reshape(bp, sp, D_MODEL), xs.reshape(bs, ls, D_MODEL),
            st('conv_p'), kv_p[0], kv_p[1], jnp.swapaxes(st('flf_p'), 2, 3), hg_p,
            kv_p[2], kv_p[3], mem_p[0], mem_p[1],
            st('conv_s'), kv_s['fk'], kv_s['fv'], st('flf_s'), hg_s,
            kv_s['sk'], kv_s['sv'])
```

```python
import functools

import jax
import jax.numpy as jnp
from jax import lax
from jax.experimental import pallas as pl
from jax.experimental.pallas import tpu as pltpu

F32 = jnp.float32
BF16 = jnp.bfloat16

D_MODEL = 1024
G = 256
H = 4
HD = 64
CONV_W = 31
HALO = CONV_W - 1
N_MEM = 256
MEM_HEADS = 4
MEM_HD = 256
D_FF = 4096
EPS = 1e-6
NEG_BIG = -1e30
SUB = 8
CHUNK = 64
SUB_SHIFT = SUB.bit_length() - 1
CHUNK_SHIFT = CHUNK.bit_length() - 1
HD_SHIFT = HD.bit_length() - 1
FF_PAD = 128
LANES = 128
SB_BLOCK = 256
ATTN_GROUPS = 8
ATTN_GROUP_MIN_ROWS = 256
LOG2E = 1.4426950408889634
SOFTPLUS_LINEAR = 64.0
QK_SCALE = HD ** -0.5 * LOG2E
VMEM_LIMIT_BYTES = 52 * 1024 * 1024


def _params(*sem):
    return pltpu.CompilerParams(dimension_semantics=sem, vmem_limit_bytes=VMEM_LIMIT_BYTES)


def _logsig(x):
    return jnp.minimum(x, 0.0) - jnp.log(1.0 + jnp.exp(-jnp.abs(x)))


def _sigmoid(x):
    return 1.0 / (1.0 + jnp.exp(-x))


def _rms(x, g):
    return x * lax.rsqrt(jnp.mean(x * x, axis=-1, keepdims=True) + EPS) * g


def _dot(a, b):
    return jnp.dot(a, b, preferred_element_type=F32)


def _dot_nt(a, b):
    return lax.dot_general(a, b, (((1,), (1,)), ((), ())), preferred_element_type=F32)


def _dot_tn(a, b):
    return lax.dot_general(a, b, (((0,), (0,)), ((), ())), preferred_element_type=F32)


def _stack_heads(x, lane_head):
    return jnp.concatenate([jnp.where(lane_head == h, x, jnp.zeros_like(x)) for h in range(H)], axis=0)


def _unstack_heads(x4, lane_head, t):
    out = jnp.zeros((t, G), x4.dtype)
    for h in range(H):
        out = jnp.where(lane_head == h, x4[h * t:(h + 1) * t], out)
    return out


def _proj_groups(x_ref, g_ref, w_ref, bff_ref, hc_ref):
    h = _rms(x_ref[...], g_ref[...]).astype(BF16)

    def col(i, n=G):
        return _dot(h, w_ref[:, i * G:i * G + n])

    yield 'u', col(0) * _sigmoid(col(1))
    yield 'fq', col(2) * QK_SCALE
    yield 'fk', col(3)
    yield 'fv', col(4)
    yield 'hq', col(5)
    hz = col(6)
    base = hc_ref[0:1, :] + _logsig(hz)
    c1 = hc_ref[1:2, :]
    lae = jnp.maximum(c1, base) + jnp.log(1.0 + jnp.exp(-jnp.abs(c1 - base)))
    yield 'hlf', jnp.where(hc_ref[2:3, :] > 0.5, lae, base)
    yield 'hk', hc_ref[3:4, :] * _sigmoid(-hz)
    yield 'hv', col(7)
    yield 'hgate', col(8)
    yield 'sq', col(9) * QK_SCALE
    yield 'sk', col(10)
    yield 'sv', col(11)
    yield 'flf', _logsig(col(12, FF_PAD) + bff_ref[...])


_PROJ_ROW_F32 = ('u', 'hq', 'hlf', 'hk', 'hv', 'hgate')
_PROJ_ROW_BF16 = ('fq', 'sq')
_PROJ_KV = ('fk', 'fv', 'sk', 'sv')


def _proj_rows_kernel(x_ref, g_ref, w_ref, bff_ref, hc_ref, *out_refs):
    names = _PROJ_ROW_F32 + _PROJ_ROW_BF16 + ('flf',) + tuple(n + s for n in _PROJ_KV for s in ('', '_b'))
    refs = dict(zip(names, out_refs))
    for name, val in _proj_groups(x_ref, g_ref, w_ref, bff_ref, hc_ref):
        if name in _PROJ_KV:
            refs[name][...] = val
            refs[name + '_b'][...] = val.astype(BF16)
        else:
            refs[name][...] = val.astype(refs[name].dtype)


def _proj_rows(x, g, w, bff, hc, tt):
    t = x.shape[0]
    row = lambda i: (i, 0)
    fix = lambda i: (0, 0)
    f32o = jax.ShapeDtypeStruct((t, G), F32)
    b16o = jax.ShapeDtypeStruct((t, G), BF16)
    outs = ([f32o] * len(_PROJ_ROW_F32) + [b16o] * len(_PROJ_ROW_BF16)
            + [jax.ShapeDtypeStruct((t, FF_PAD), F32)] + [f32o, b16o] * len(_PROJ_KV))
    names = _PROJ_ROW_F32 + _PROJ_ROW_BF16 + ('flf',) + tuple(n + s for n in _PROJ_KV for s in ('', '_b'))
    res = pl.pallas_call(
        _proj_rows_kernel,
        grid=(t // tt,),
        in_specs=[pl.BlockSpec((tt, D_MODEL), row), pl.BlockSpec((1, D_MODEL), fix),
                  pl.BlockSpec((D_MODEL, w.shape[1]), fix), pl.BlockSpec((1, FF_PAD), fix),
                  pl.BlockSpec((4, G), fix)],
        out_specs=[pl.BlockSpec((tt, o.shape[1]), row) for o in outs],
        out_shape=outs,
        compiler_params=_params("parallel"),
        name="proj_rows",
    )(x, g, w, bff, hc)
    return dict(zip(names, res))


_SEQ_ROW_F32 = ('u',)
_SEQ_ROW_BF16 = ('fq', 'sq', 'conv_o', 'hg_o')
_SEQ_HGRN_IN = ('hq', 'hk', 'hv', 'hlf', 'hgate')
_SEQ_NAMES = (_SEQ_ROW_F32 + _SEQ_ROW_BF16 + ('flf',)
              + tuple(n + s for n in _PROJ_KV for s in ('_t', '_tb')) + ('st_out',))


def _proj_seq_kernel(x_ref, g_ref, w_ref, bff_ref, hc_ref, cw_ref, cp_ref, hn_ref, *rest, n_alias, nt, tt):
    rest = rest[n_alias:]
    refs = dict(zip(_SEQ_NAMES, rest[:len(_SEQ_NAMES)]))
    ext_ref, st_ref, kpad, vpad, bpad = rest[len(_SEQ_NAMES):]
    i = pl.program_id(0) % nt

    @pl.when(i == 0)
    def _():
        _conv_start(ext_ref, jnp.zeros((HALO, G), F32), tt)
        st_ref[...] = jnp.zeros(st_ref.shape, F32)

    vals = {}
    for name, val in _proj_groups(x_ref, g_ref, w_ref, bff_ref, hc_ref):
        if name in _PROJ_KV:
            vt = val.T
            refs[name + '_t'][0, 0] = vt
            refs[name + '_tb'][0, 0] = vt.astype(BF16)
        elif name == 'flf':
            refs[name][0] = val.T[0:8, :]
        elif name in _SEQ_HGRN_IN:
            vals[name] = val
        else:
            refs[name][...] = val.astype(refs[name].dtype)
            if name == 'u':
                vals[name] = val
    _conv_tile(vals['u'], cw_ref, cp_ref, refs['conv_o'], ext_ref, tt)
    _hgrn_tile(vals['hq'], vals['hk'], vals['hv'], vals['hlf'], vals['hgate'], hn_ref,
               refs['hg_o'], st_ref, kpad, vpad, bpad, tt)

    @pl.when(i == nt - 1)
    def _():
        refs['st_out'][0] = st_ref[...]


def _proj_seq(x, g, w, bff, hc, cw, cp, hn, bufs, *, layer, depth, b, l, tt):
    t = x.shape[0]
    nt = l // tt
    row = lambda i: (i, 0)
    fix = lambda i: (0, 0)
    f32o = jax.ShapeDtypeStruct((t, G), F32)
    b16o = jax.ShapeDtypeStruct((t, G), BF16)
    stacked = jax.ShapeDtypeStruct((depth, b, G, l), F32)
    tiled = jax.ShapeDtypeStruct((b, nt, G, tt), BF16)
    n_rows = len(_SEQ_ROW_F32) + len(_SEQ_ROW_BF16)
    outs = ([f32o] * len(_SEQ_ROW_F32) + [b16o] * len(_SEQ_ROW_BF16)
            + [jax.ShapeDtypeStruct((b, 8, l), F32)] + [stacked, tiled] * len(_PROJ_KV)
            + [jax.ShapeDtypeStruct((b, G, G), F32)])
    ospecs = ([pl.BlockSpec((tt, G), row)] * n_rows
              + [pl.BlockSpec((1, 8, tt), lambda i: (i // nt, 0, i % nt))]
              + [pl.BlockSpec((1, 1, G, tt), lambda i: (layer, i // nt, 0, i % nt)),
                 pl.BlockSpec((1, 1, G, tt), lambda i: (i // nt, i % nt, 0, 0))] * len(_PROJ_KV)
              + [pl.BlockSpec((1, G, G), lambda i: (i // nt, 0, 0))])
    n_in = 8
    first_stacked = n_rows + 1
    alias_args = list(bufs)
    aliases = {n_in + k: first_stacked + 2 * k for k in range(len(_PROJ_KV))}
    res = pl.pallas_call(
        functools.partial(_proj_seq_kernel, n_alias=len(alias_args), nt=nt, tt=tt),
        grid=(t // tt,),
        in_specs=[pl.BlockSpec((tt, D_MODEL), row), pl.BlockSpec((1, D_MODEL), fix),
                  pl.BlockSpec((D_MODEL, w.shape[1]), fix), pl.BlockSpec((1, FF_PAD), fix),
                  pl.BlockSpec((4, G), fix), pl.BlockSpec((CONV_W, G), fix), pl.BlockSpec((3, G), fix),
                  pl.BlockSpec((1, G), fix)] + [pl.BlockSpec(memory_space=pl.ANY)] * len(alias_args),
        out_specs=ospecs,
        out_shape=outs,
        input_output_aliases=aliases,
        scratch_shapes=[pltpu.VMEM((tt + 40, G), F32), pltpu.VMEM((G, G), F32)]
                       + [pltpu.VMEM((tt + SUB, G), F32)] * 3,
        compiler_params=_params("arbitrary"),
        name="proj_seq",
    )(x, g, w, bff, hc, cw, cp, hn, *alias_args)
    return dict(zip(_SEQ_NAMES, res))


def _cumsum_kernel(x_ref, o_ref, *, scale):
    x = x_ref[...]
    n = x.shape[1]
    lane = lax.broadcasted_iota(jnp.int32, x.shape, 1)
    sh = 1
    while sh < n:
        x = x + jnp.where(lane >= sh, pltpu.roll(x, sh, axis=1), 0.0)
        sh *= 2
    o_ref[...] = x * scale


def _cumsum_lanes(x, scale):
    return pl.pallas_call(
        functools.partial(_cumsum_kernel, scale=scale),
        out_shape=jax.ShapeDtypeStruct(x.shape, F32),
        compiler_params=pltpu.CompilerParams(vmem_limit_bytes=VMEM_LIMIT_BYTES),
        name="cumsum",
    )(x)


CONV_BASE = 32 - HALO


def _conv_start(ext_ref, hist, tt):
    ext_ref[0:CONV_BASE, :] = jnp.zeros((CONV_BASE, G), F32)
    ext_ref[CONV_BASE:32, :] = hist
    ext_ref[32 + tt:40 + tt, :] = jnp.zeros((8, G), F32)


def _conv_tile(u, w_ref, p_ref, o_ref, ext_ref, tt):
    base = CONV_BASE
    ext_ref[32:32 + tt, :] = u
    acc = None
    for r in range(8):
        part = None
        for m in range((base + CONV_W - 1) // 8 + 1):
            j = 8 * m + r - base
            if 0 <= j < CONV_W:
                term = w_ref[j:j + 1, :] * ext_ref[8 * m:8 * m + tt + 8, :]
                part = term if part is None else part + term
        part = part[r:r + tt, :]
        acc = part if acc is None else acc + part
    yf = acc + p_ref[0:1, :]
    mu = jnp.mean(yf, axis=-1, keepdims=True)
    d = yf - mu
    var = jnp.mean(d * d, axis=-1, keepdims=True)
    yn = d * lax.rsqrt(var + EPS) * p_ref[1:2, :] + p_ref[2:3, :]
    o_ref[...] = (yn * _sigmoid(yn)).astype(BF16)
    ext_ref[base:32, :] = ext_ref[tt + base:tt + 32, :]


def _conv_kernel(u_ref, hist_ref, w_ref, p_ref, o_ref, ext_ref, *, tt):
    @pl.when(pl.program_id(1) == 0)
    def _():
        _conv_start(ext_ref, hist_ref[0], tt)

    _conv_tile(u_ref[...], w_ref, p_ref, o_ref, ext_ref, tt)


def _conv(u, hist, w, p, *, row_off, b, l, tt):
    nt = l // tt
    off = row_off // tt
    return pl.pallas_call(
        functools.partial(_conv_kernel, tt=tt),
        grid=(b, nt),
        in_specs=[pl.BlockSpec((tt, G), lambda bi, i: (off + bi * nt + i, 0)),
                  pl.BlockSpec((1, HALO, G), lambda bi, i: (bi, 0, 0)),
                  pl.BlockSpec((CONV_W, G), lambda bi, i: (0, 0)),
                  pl.BlockSpec((3, G), lambda bi, i: (0, 0))],
        out_specs=pl.BlockSpec((tt, G), lambda bi, i: (bi * nt + i, 0)),
        out_shape=jax.ShapeDtypeStruct((b * l, G), BF16),
        scratch_shapes=[pltpu.VMEM((tt + 40, G), F32)],
        compiler_params=_params("arbitrary", "arbitrary"),
        name="conv",
    )(u, hist, w, p)


def _lanes(x, w):
    if w < LANES:
        return x[:, :w]
    return x if w == LANES else jnp.concatenate([x] * (w // LANES), axis=1)


def _scores(q4, k, t_layout):
    return _dot(q4, k) if t_layout else _dot_nt(q4, k)


def _weighted(p4, v, t_layout):
    return _dot_nt(p4, v) if t_layout else _dot(p4, v)


def _fox_kernel(*refs, tq, tkp, n_past, cur_t):
    if n_past:
        (q_ref, kc_ref, vc_ref, ckc_ref, cq_ref, kp_ref, vp_ref, ckp_ref,
         o_ref, m_ref, l_ref, acc_ref) = refs
    else:
        q_ref, kc_ref, vc_ref, ckc_ref, cq_ref, o_ref, m_ref, l_ref, acc_ref = refs
    i = pl.program_id(1)
    lane_head = lax.broadcasted_iota(jnp.int32, (tq, G), 1) >> HD_SHIFT
    q4 = _stack_heads(q_ref[...], lane_head)
    cq = cq_ref[...]
    cqb = [jnp.broadcast_to(cq[:, h:h + 1], (tq, LANES)) for h in range(H)]
    tri = (lax.broadcasted_iota(jnp.int32, (tq, tq), 1)
           <= lax.broadcasted_iota(jnp.int32, (tq, tq), 0))
    m_ref[...] = jnp.full(m_ref.shape, NEG_BIG, F32)
    l_ref[...] = jnp.zeros(l_ref.shape, F32)
    acc_ref[...] = jnp.zeros(acc_ref.shape, F32)

    def step(k, v, ck, masked, t_layout):
        s = _scores(q4, k, t_layout)
        w = s.shape[1]
        ps = []
        for h in range(H):
            sl = slice(h * tq, (h + 1) * tq)
            sh = s[sl] - ck[h:h + 1, :]
            if masked:
                sh = jnp.where(tri, sh, NEG_BIG)
            m_old = m_ref[sl]
            m_new = jnp.maximum(m_old, jnp.max(sh, axis=-1, keepdims=True) + cqb[h])
            p = jnp.exp2(sh - _lanes(m_new - cqb[h], w))
            alpha = jnp.exp2(m_old - m_new)
            l_ref[sl] = alpha * l_ref[sl] + jnp.sum(p, axis=-1, keepdims=True)
            m_ref[sl] = m_new
            acc_ref[sl] = acc_ref[sl] * _lanes(alpha, G)
            ps.append(p.astype(BF16))
        acc_ref[...] += _weighted(jnp.concatenate(ps, axis=0), v, t_layout)

    for j in range(n_past):
        sl = slice(j * tkp, (j + 1) * tkp)
        step(kp_ref[0, 0, :, sl].astype(BF16), vp_ref[0, 0, :, sl].astype(BF16), ckp_ref[0, j], False, True)

    if cur_t:
        def cur_body(j, c):
            step(kc_ref[0, j], vc_ref[0, j], ckc_ref[0, j], False, True)
            return c
        lax.fori_loop(0, i, cur_body, 0)
        step(kc_ref[0, i], vc_ref[0, i], ckc_ref[0, i], True, True)
    else:
        step(kc_ref[...], vc_ref[...], ckc_ref[0, 0], True, False)

    acc4 = jnp.concatenate(
        [acc_ref[h * tq:(h + 1) * tq] * _lanes(1.0 / l_ref[h * tq:(h + 1) * tq], G) for h in range(H)],
        axis=0)
    o_ref[...] = _unstack_heads(acc4, lane_head, tq).astype(BF16)


def _sb_kernel(*refs, tq, tkp, n_past, cur_t):
    if n_past:
        q_ref, kc_ref, vc_ref, kp_ref, vp_ref, o_ref, r_ref, acc_ref = refs
    else:
        q_ref, kc_ref, vc_ref, o_ref, r_ref, acc_ref = refs
    i = pl.program_id(1)
    lane_head = lax.broadcasted_iota(jnp.int32, (tq, G), 1) >> HD_SHIFT
    q4 = _stack_heads(q_ref[...], lane_head)
    r_ref[...] = jnp.zeros(r_ref.shape, F32)
    acc_ref[...] = jnp.zeros(acc_ref.shape, F32)

    def later_matrix(n):
        later = (lax.broadcasted_iota(jnp.int32, (n, n), 0) > lax.broadcasted_iota(jnp.int32, (n, n), 1))
        return jnp.where(later, 1.0, 0.0).astype(BF16)

    widths = {min(tq, SB_BLOCK)} | ({min(tkp, SB_BLOCK)} if n_past else set())
    later = {n: later_matrix(n) for n in widths}

    tri = (lax.broadcasted_iota(jnp.int32, (tq, tq), 1)
           < lax.broadcasted_iota(jnp.int32, (tq, tq), 0))
    valid = jnp.concatenate([tri] * H, axis=0)
    groups = ATTN_GROUPS if H * tq // ATTN_GROUPS >= ATTN_GROUP_MIN_ROWS else 1

    def step(k, v, masked, t_layout):
        for g in range(groups):
            rows = slice(g * (H * tq // groups), (g + 1) * (H * tq // groups))
            z = _scores(q4[rows], k, t_layout)
            w = z.shape[1]
            sp = jnp.where(z > SOFTPLUS_LINEAR, z, jnp.log2(1.0 + jnp.exp2(z)))
            u = jnp.where(valid[rows], sp, 0.0) if masked else sp
            bw = min(w, SB_BLOCK)
            r_run = r_ref[rows]
            rests = [None] * (w // bw)
            for blk in reversed(range(w // bw)):
                ub = u[:, blk * bw:(blk + 1) * bw]
                rests[blk] = _dot(ub.astype(BF16), later[bw]) + _lanes(r_run, bw)
                r_run = r_run + jnp.sum(ub, axis=-1, keepdims=True)
            rest = rests[0] if len(rests) == 1 else jnp.concatenate(rests, axis=1)
            a = jnp.exp2(z - sp - rest)
            if masked:
                a = jnp.where(valid[rows], a, 0.0)
            acc_ref[rows] += _weighted(a.astype(BF16), v, t_layout)
            r_ref[rows] = r_run

    if cur_t:
        step(kc_ref[0, i], vc_ref[0, i], True, True)

        def cur_body(n, c):
            j = i - 1 - n
            step(kc_ref[0, j], vc_ref[0, j], False, True)
            return c
        lax.fori_loop(0, i, cur_body, 0)
    else:
        step(kc_ref[...], vc_ref[...], True, False)

    for j in reversed(range(n_past)):
        sl = slice(j * tkp, (j + 1) * tkp)
        step(kp_ref[0, 0, :, sl].astype(BF16), vp_ref[0, 0, :, sl].astype(BF16), False, True)

    o_ref[...] = _unstack_heads(acc_ref[...], lane_head, tq).astype(BF16)


def _attention(kind, q, kc, vc, extra, past, *, layer, b, l, tq, tkp):
    nq = l // tq
    cur_t = kc.ndim == 4
    assert cur_t or nq == 1
    if cur_t:
        kv_spec = pl.BlockSpec((1, nq, G, tq), lambda bi, i: (bi, 0, 0, 0))
    else:
        kv_spec = pl.BlockSpec((l, G), lambda bi, i: (bi, 0))
    in_specs = [pl.BlockSpec((tq, G), lambda bi, i: (bi * nq + i, 0)), kv_spec, kv_spec]
    args = [q, kc, vc]
    if kind == 'fox':
        ckc, cq = extra
        in_specs += [pl.BlockSpec((1, nq, H, tq), lambda bi, i: (bi, 0, 0, 0)),
                     pl.BlockSpec((tq, H), lambda bi, i: (bi * nq + i, 0))]
        args += [ckc, cq]
    n_past = 0
    if past is not None:
        plen = past[0].shape[3]
        n_past = plen // tkp
        cache_spec = pl.BlockSpec((1, 1, G, plen), lambda bi, i: (layer, bi, 0, 0))
        in_specs += [cache_spec, cache_spec]
        args += [past[0], past[1]]
        if kind == 'fox':
            in_specs.append(pl.BlockSpec((1, n_past, H, tkp), lambda bi, i: (bi, 0, 0, 0)))
            args.append(past[2])
    stat = pltpu.VMEM((H * tq, LANES), F32)
    acc = pltpu.VMEM((H * tq, G), F32)
    body = _fox_kernel if kind == 'fox' else _sb_kernel
    return pl.pallas_call(
        functools.partial(body, tq=tq, tkp=tkp, n_past=n_past, cur_t=cur_t),
        grid=(b, nq),
        in_specs=in_specs,
        out_specs=pl.BlockSpec((tq, G), lambda bi, i: (bi * nq + i, 0)),
        out_shape=jax.ShapeDtypeStruct((b * l, G), BF16),
        scratch_shapes=[stat, stat, acc] if kind == 'fox' else [stat, acc],
        compiler_params=_params("parallel", "arbitrary"),
        name=kind,
    )(*args)


def _hgrn_tile(q, kk, v, lf, gate, hn_ref, o_ref, st_ref, kpad, vpad, bpad, tt):
    nc = tt // CHUNK
    row = lax.broadcasted_iota(jnp.int32, (tt, G), 0)
    r64 = row & (CHUNK - 1)
    rsub = row & (SUB - 1)
    lane_head = lax.broadcasted_iota(jnp.int32, (tt, G), 1) >> HD_SHIFT
    same_head = (lax.broadcasted_iota(jnp.int32, (G, G), 0) >> HD_SHIFT
                 == lax.broadcasted_iota(jnp.int32, (G, G), 1) >> HD_SHIFT)
    bd = jnp.where(same_head, 1.0, 0.0).astype(BF16)

    b = lf * LOG2E
    sh = 1
    while sh < CHUNK:
        b = b + jnp.where(r64 >= sh, pltpu.roll(b, sh, axis=0), 0.0)
        sh *= 2

    def chunk_row(r):
        return jnp.concatenate(
            [jnp.broadcast_to(b[c * CHUNK + r:c * CHUNK + r + 1, :], (CHUNK, G)) for c in range(nc)], axis=0)

    zpad = jnp.zeros((SUB, G), F32)
    kpad[0:SUB, :] = zpad
    vpad[0:SUB, :] = zpad
    bpad[0:SUB, :] = zpad
    kpad[SUB:SUB + tt, :] = kk
    vpad[SUB:SUB + tt, :] = v
    bpad[SUB:SUB + tt, :] = b
    o = jnp.zeros((tt, G), F32)
    for d in range(SUB):
        ks = kpad[SUB - d:SUB - d + tt, :]
        vs = vpad[SUB - d:SUB - d + tt, :]
        bs = bpad[SUB - d:SUB - d + tt, :]
        p = q * ks * jnp.exp2(jnp.minimum(b - bs, 0.0))
        p = jnp.where(rsub >= d, p, 0.0)
        o = o + _dot(p.astype(BF16), bd) * vs

    nsub = CHUNK // SUB
    refs_b = [chunk_row(SUB * n - 1) for n in range(1, nsub)]
    sub = r64 >> SUB_SHIFT
    rq = b
    for n, rb in enumerate(refs_b, start=1):
        rq = jnp.where(sub == n, rb, rq)
    qt = q * jnp.exp2(jnp.minimum(b - rq, 0.0))
    gr = SUB * nc
    lane_head_g = lax.broadcasted_iota(jnp.int32, (gr, G), 1) >> HD_SHIFT
    g_row = lax.broadcasted_iota(jnp.int32, (gr, tt), 0)
    g_col = lax.broadcasted_iota(jnp.int32, (gr, tt), 1)
    key_sub = jnp.where(g_row >> SUB_SHIFT == g_col >> CHUNK_SHIFT, (g_col & (CHUNK - 1)) >> SUB_SHIFT, nsub)
    key_sub4 = jnp.concatenate([key_sub] * H, axis=0)
    a_all = []
    for n, rb in enumerate(refs_b, start=1):
        qn = jnp.concatenate([qt[c * CHUNK + SUB * n:c * CHUNK + SUB * (n + 1)] for c in range(nc)], axis=0)
        q4n = _stack_heads(qn, lane_head_g).astype(BF16)
        kt = (kk * jnp.exp2(jnp.minimum(rb - b, 0.0))).astype(BF16)
        a_all.append(jnp.where(key_sub4 < n, _dot_nt(q4n, kt), 0.0).astype(BF16))
    vb = v.astype(BF16)
    o_all = _dot(jnp.concatenate(a_all, axis=0), vb)
    o_sub = [_unstack_heads(o_all[(n - 1) * H * gr:n * H * gr], lane_head_g, gr) for n in range(1, nsub)]
    pieces = []
    for c in range(nc):
        pieces.append(jnp.zeros((SUB, G), F32))
        pieces += [o_sub[n - 1][c * SUB:(c + 1) * SUB] for n in range(1, nsub)]
    o = o + jnp.concatenate(pieces, axis=0)

    blast = chunk_row(CHUNK - 1)
    qs = (q * jnp.exp2(b)).astype(BF16)
    kd = (kk * jnp.exp2(jnp.minimum(blast - b, 0.0))).astype(BF16)
    dec = jnp.exp2(blast)
    outs = []
    for c in range(nc):
        sl = slice(c * CHUNK, (c + 1) * CHUNK)
        st = st_ref[...]
        outs.append(_dot_nt(qs[sl], st.astype(BF16)))
        upd = _dot_tn(vb[sl], kd[sl])
        st_ref[...] = st * dec[c * CHUNK:c * CHUNK + 1, :] + jnp.where(same_head, upd, 0.0)
    o = o + jnp.concatenate(outs, axis=0)

    sq = o * o
    hi = sq.astype(BF16)
    lo = (sq - hi.astype(F32)).astype(BF16)
    ms = (_dot(hi, bd) + _dot(lo, bd)) * (1.0 / HD)
    o_ref[...] = (o * lax.rsqrt(ms + EPS) * hn_ref[...] * (gate * _sigmoid(gate))).astype(BF16)


def _hgrn_kernel(*refs, tt, has_init):
    if has_init:
        (q_ref, k_ref, v_ref, lf_ref, gate_ref, hn_ref, st0_ref,
         o_ref, sto_ref, st_ref, kpad, vpad, bpad) = refs
    else:
        (q_ref, k_ref, v_ref, lf_ref, gate_ref, hn_ref,
         o_ref, sto_ref, st_ref, kpad, vpad, bpad) = refs
    i = pl.program_id(1)

    @pl.when(i == 0)
    def _():
        if has_init:
            st_ref[...] = st0_ref[0]
        else:
            st_ref[...] = jnp.zeros(st_ref.shape, F32)

    _hgrn_tile(q_ref[...], k_ref[...], v_ref[...], lf_ref[...], gate_ref[...], hn_ref,
               o_ref, st_ref, kpad, vpad, bpad, tt)

    @pl.when(i == pl.num_programs(1) - 1)
    def _():
        sto_ref[0] = st_ref[...]


def _hgrn(q, k, v, lf, gate, hn, st0, *, row_off, b, l, tt):
    nt = l // tt
    off = row_off // tt
    tok = pl.BlockSpec((tt, G), lambda bi, i: (off + bi * nt + i, 0))
    in_specs = [tok, tok, tok, tok, tok, pl.BlockSpec((1, G), lambda bi, i: (0, 0))]
    args = [q, k, v, lf, gate, hn]
    if st0 is not None:
        in_specs.append(pl.BlockSpec((1, G, G), lambda bi, i: (bi, 0, 0)))
        args.append(st0)
    return pl.pallas_call(
        functools.partial(_hgrn_kernel, tt=tt, has_init=st0 is not None),
        grid=(b, nt),
        in_specs=in_specs,
        out_specs=[pl.BlockSpec((tt, G), lambda bi, i: (bi * nt + i, 0)),
                   pl.BlockSpec((1, G, G), lambda bi, i: (bi, 0, 0))],
        out_shape=[jax.ShapeDtypeStruct((b * l, G), BF16), jax.ShapeDtypeStruct((b, G, G), F32)],
        scratch_shapes=[pltpu.VMEM((G, G), F32)] + [pltpu.VMEM((tt + SUB, G), F32)] * 3,
        compiler_params=_params("arbitrary", "arbitrary"),
        name="hgrn",
    )(*args)


def _c1_kernel(x_ref, m0_ref, m1_ref, m2_ref, m3_ref, wo_ref, g_ref, wq_ref, x1_ref, q_ref):
    acc = x_ref[...]
    for p, m_ref in enumerate((m0_ref, m1_ref, m2_ref, m3_ref)):
        acc = acc + _dot(m_ref[...], wo_ref[p * G:(p + 1) * G, :])
    x1_ref[...] = acc
    h = _rms(acc, g_ref[...]).astype(BF16)
    q_ref[...] = (_dot(h, wq_ref[...]) * (MEM_HD ** -0.5)).astype(BF16)


def _c1(x, mix, wo, g, wq, tt):
    t = x.shape[0]
    row = lambda i: (i, 0)
    fix = lambda i: (0, 0)
    return pl.pallas_call(
        _c1_kernel,
        grid=(t // tt,),
        in_specs=[pl.BlockSpec((tt, D_MODEL), row)] + [pl.BlockSpec((tt, G), row)] * 4
                 + [pl.BlockSpec((D_MODEL, D_MODEL), fix), pl.BlockSpec((1, D_MODEL), fix),
                    pl.BlockSpec((D_MODEL, D_MODEL), fix)],
        out_specs=[pl.BlockSpec((tt, D_MODEL), row), pl.BlockSpec((tt, D_MODEL), row)],
        out_shape=[jax.ShapeDtypeStruct((t, D_MODEL), F32), jax.ShapeDtypeStruct((t, D_MODEL), BF16)],
        compiler_params=_params("parallel"),
        name="outproj_memq",
    )(x, *mix, wo, g, wq)


def _memkv_kernel(m_ref, wk_ref, wv_ref, k_ref, kb_ref, v_ref, vb_ref):
    m = m_ref[...].astype(BF16)
    k = _dot(m, wk_ref[...])
    k_ref[...] = k
    kb_ref[...] = k.astype(BF16)
    v = _dot(m, wv_ref[...])
    v_ref[...] = v
    vb_ref[...] = v.astype(BF16)


def _memkv(mem, wk, wv, tt):
    t = mem.shape[0]
    row = lambda i: (i, 0)
    fix = lambda i: (0, 0)
    f32o = jax.ShapeDtypeStruct((t, D_MODEL), F32)
    b16o = jax.ShapeDtypeStruct((t, D_MODEL), BF16)
    return pl.pallas_call(
        _memkv_kernel,
        grid=(t // tt,),
        in_specs=[pl.BlockSpec((tt, D_MODEL), row), pl.BlockSpec((D_MODEL, D_MODEL), fix),
                  pl.BlockSpec((D_MODEL, D_MODEL), fix)],
        out_specs=[pl.BlockSpec((tt, D_MODEL), row)] * 4,
        out_shape=[f32o, b16o, f32o, b16o],
        compiler_params=_params("parallel"),
        name="memkv",
    )(mem, wk, wv)


def _memattn_kernel(q_ref, k_ref, v_ref, o_ref):
    outs = []
    for h in range(MEM_HEADS):
        sl = slice(h * MEM_HD, (h + 1) * MEM_HD)
        s = _dot_nt(q_ref[:, sl], k_ref[:, sl].astype(BF16))
        p = jnp.exp(s - jnp.max(s, axis=-1, keepdims=True))
        den = jnp.sum(p, axis=-1, keepdims=True)
        outs.append(_dot(p.astype(BF16), v_ref[:, sl].astype(BF16)) * (1.0 / den))
    o_ref[...] = jnp.concatenate(outs, axis=-1).astype(BF16)


def _memattn(q, k, v, *, row_off, b, l, tq):
    nq = l // tq
    off = row_off // tq
    return pl.pallas_call(
        _memattn_kernel,
        grid=(b, nq),
        in_specs=[pl.BlockSpec((tq, D_MODEL), lambda bi, i: (off + bi * nq + i, 0)),
                  pl.BlockSpec((N_MEM, D_MODEL), lambda bi, i: (bi, 0)),
                  pl.BlockSpec((N_MEM, D_MODEL), lambda bi, i: (bi, 0))],
        out_specs=pl.BlockSpec((tq, D_MODEL), lambda bi, i: (bi * nq + i, 0)),
        out_shape=jax.ShapeDtypeStruct((b * l, D_MODEL), BF16),
        compiler_params=_params("parallel", "parallel"),
        name="memattn",
    )(q, k, v)


def _c3_kernel(x1_ref, o_ref, wmo_ref, g_ref, wup_ref, wdn_ref, gf_ref, y_ref, *, final, tf):
    x2 = x1_ref[...] + _dot(o_ref[...], wmo_ref[...])
    h = _rms(x2, g_ref[...]).astype(BF16)
    x3 = x2
    for f in range(D_FF // tf):
        a = jnp.maximum(_dot(h, wup_ref[:, f * tf:(f + 1) * tf]), 0.0)
        x3 = x3 + _dot((a * a).astype(BF16), wdn_ref[f * tf:(f + 1) * tf, :])
    y_ref[...] = _rms(x3, gf_ref[...]) if final else x3


def _c3(x1, o, wmo, g, wup, wdn, gf, *, tt, tf, final):
    t = x1.shape[0]
    row = lambda i: (i, 0)
    fix = lambda i: (0, 0)
    resident = lambda shape: pl.BlockSpec(shape, fix, pipeline_mode=pl.Buffered(1))
    return pl.pallas_call(
        functools.partial(_c3_kernel, final=final, tf=tf),
        grid=(t // tt,),
        in_specs=[pl.BlockSpec((tt, D_MODEL), row), pl.BlockSpec((tt, D_MODEL), row),
                  resident((D_MODEL, D_MODEL)), pl.BlockSpec((1, D_MODEL), fix),
                  resident((D_MODEL, D_FF)), resident((D_FF, D_MODEL)),
                  pl.BlockSpec((1, D_MODEL), fix)],
        out_specs=pl.BlockSpec((tt, D_MODEL), row),
        out_shape=jax.ShapeDtypeStruct((t, D_MODEL), F32),
        compiler_params=_params("parallel"),
        name="memout_mlp",
    )(x1, o, wmo, g, wup, wdn, gf)


def _block_diag_t(s):
    b = s.shape[0]
    eye = jnp.eye(H, dtype=s.dtype)
    st = jnp.swapaxes(s, 2, 3)
    return jnp.einsum('bhvk,hg->bhvgk', st, eye).reshape(b, G, G)


def _unblock_diag_t(st):
    b = st.shape[0]
    s5 = st.reshape(b, H, HD, H, HD)
    d = jnp.stack([s5[:, h, :, h, :] for h in range(H)], axis=1)
    return jnp.swapaxes(d, 2, 3)


def _row_cumsum(lf, tile, scale):
    b, s, _ = lf.shape
    sp = -(-s // tile) * tile
    x = jnp.swapaxes(lf, 1, 2).reshape(b * H, s)
    x = jnp.pad(x, ((0, 0), (0, sp - s)))
    return _cumsum_lanes(x, scale).reshape(b, H, sp)[:, :, :s]


def _tiles(c, n, t):
    b = c.shape[0]
    return jnp.swapaxes(c.reshape(b, H, n, t), 1, 2)


def kernel(x_prompt, x_sample, mem_prompt, cache_conv, cache_fox_k, cache_fox_v, cache_fox_logf,
           state_hgrn, cache_sb_k, cache_sb_v, cache_mem_k, cache_mem_v, norm_mix, w_in, b_fox_f,
           conv_w, conv_b, conv_ln_g, conv_ln_b, hgrn_lb, hgrn_norm, w_out, norm_mem, w_mq, w_mk,
           w_mv, w_mo, norm_ffn, w_up, w_down, norm_final):
    bp, sp, _ = x_prompt.shape
    bs, ls, _ = x_sample.shape
    depth = w_in.shape[0]
    past = cache_fox_k.shape[2]
    tp = bp * sp
    ts = bs * ls
    tok = 512
    tq_p = 512
    tk_past = 512

    sm = jax.nn.softmax(hgrn_lb.astype(F32), axis=0)
    lower = jnp.clip(jnp.cumsum(sm, axis=0) - sm[0], 0.0, 1.0 - 1e-6)
    pos = lower > 0.0
    hconst = jnp.stack([jnp.log1p(-lower), jnp.log(jnp.where(pos, lower, 1.0)),
                        pos.astype(F32), 1.0 - lower], axis=1)

    o_ff = 5 * G
    w_in_r = jnp.concatenate(
        [w_in[:, :, :o_ff], w_in[:, :, o_ff + H:],
         jnp.pad(w_in[:, :, o_ff:o_ff + H], ((0, 0), (0, 0), (0, FF_PAD - H)))], axis=-1).astype(BF16)
    bff = jnp.pad(b_fox_f, ((0, 0), (0, FF_PAD - H)))[:, None, :]
    w_out_b = w_out.astype(BF16)
    w_mq_b = w_mq.astype(BF16)
    w_mk_b = w_mk.astype(BF16)
    w_mv_b = w_mv.astype(BF16)
    w_mo_b = w_mo.astype(BF16)
    w_up_b = w_up.astype(BF16)
    w_down_b = w_down.astype(BF16)
    hn = jnp.tile(hgrn_norm, (1, H))[:, None, :]
    conv_p3 = jnp.stack([conv_b, conv_ln_g, conv_ln_b], axis=1)

    def seq_last(c):
        return jnp.transpose(c, (0, 1, 3, 4, 2)).reshape(depth, bs, G, past)

    fkc, fvc, skc, svc = (seq_last(c) for c in (cache_fox_k, cache_fox_v, cache_sb_k, cache_sb_v))

    st0_all = _block_diag_t(state_hgrn.astype(F32).reshape(depth * bs, H, HD, HD)).reshape(depth, bs, G, G)
    xp = x_prompt.reshape(tp, D_MODEL)
    xs = x_sample.reshape(ts, D_MODEL)
    mem2 = mem_prompt.reshape(bp * N_MEM, D_MODEL)
    bufs = tuple(jnp.zeros((depth, bp, G, sp), F32) for _ in _PROJ_KV)

    outs = {n: [] for n in ('conv_p', 'flf_p', 'hg_p', 'mk_p', 'mv_p',
                            'conv_s', 'fk_s', 'fv_s', 'flf_s', 'hg_s', 'sk_s', 'sv_s')}
    for l in range(depth):
        pp = _proj_seq(xp, norm_mix[l][None], w_in_r[l], bff[l], hconst[l], conv_w[l], conv_p3[l], hn[l],
                       bufs, layer=l, depth=depth, b=bp, l=sp, tt=tq_p)
        bufs = tuple(pp[n + '_t'] for n in _PROJ_KV)
        conv_op, hg_op, st_p = pp['conv_o'], pp['hg_o'], pp['st_out']
        ps = _proj_rows(xs, norm_mix[l][None], w_in_r[l], bff[l], hconst[l], tok)

        conv_os = _conv(ps['u'], cache_conv[l], conv_w[l], conv_p3[l], row_off=0, b=bs, l=ls, tt=ls)

        flf_pt = pp['flf'][:, :H, :]
        flf_s = ps['flf'][:, :H].reshape(bs, ls, H)
        c_p = _cumsum_lanes(flf_pt.reshape(bp * H, sp), LOG2E).reshape(bp, H, sp)
        c_s = _row_cumsum(jnp.concatenate([cache_fox_logf[l].astype(F32), flf_s], axis=1), LANES, LOG2E)
        fox_op = _attention('fox', pp['fq'], pp['fk_tb'], pp['fv_tb'],
                            (_tiles(c_p, sp // tq_p, tq_p), jnp.swapaxes(c_p, 1, 2).reshape(tp, H)),
                            None, layer=l, b=bp, l=sp, tq=tq_p, tkp=tk_past)
        fox_os = _attention('fox', ps['fq'], ps['fk_b'], ps['fv_b'],
                            (_tiles(c_s[:, :, past:], 1, ls), jnp.swapaxes(c_s[:, :, past:], 1, 2).reshape(ts, H)),
                            (fkc, fvc, _tiles(c_s[:, :, :past], past // tk_past, tk_past)),
                            layer=l, b=bs, l=ls, tq=ls, tkp=tk_past)

        hg_os, st_s = _hgrn(ps['hq'], ps['hk'], ps['hv'], ps['hlf'], ps['hgate'], hn[l],
                            st0_all[l], row_off=0, b=bs, l=ls, tt=ls)

        sb_op = _attention('sb', pp['sq'], pp['sk_tb'], pp['sv_tb'], (), None,
                           layer=l, b=bp, l=sp, tq=tq_p, tkp=tk_past)
        sb_os = _attention('sb', ps['sq'], ps['sk_b'], ps['sv_b'], (), (skc, svc),
                           layer=l, b=bs, l=ls, tq=ls, tkp=tk_past)

        x1p, qmp = _c1(xp, [conv_op, fox_op, hg_op, sb_op], w_out_b[l], norm_mem[l][None], w_mq_b[l], tok)
        x1s, qms = _c1(xs, [conv_os, fox_os, hg_os, sb_os], w_out_b[l], norm_mem[l][None], w_mq_b[l], tok)

        mk, mkb, mv, mvb = _memkv(mem2, w_mk_b[l], w_mv_b[l], N_MEM)
        om_p = _memattn(qmp, mkb, mvb, row_off=0, b=bp, l=sp, tq=tok)
        om_s = _memattn(qms, cache_mem_k[l].reshape(bs * N_MEM, D_MODEL),
                        cache_mem_v[l].reshape(bs * N_MEM, D_MODEL), row_off=0, b=bs, l=ls, tq=ls)

        final = l == depth - 1
        xp = _c3(x1p, om_p, w_mo_b[l], norm_ffn[l][None], w_up_b[l], w_down_b[l], norm_final[None],
                 tt=tok, tf=1024, final=final)
        xs = _c3(x1s, om_s, w_mo_b[l], norm_ffn[l][None], w_up_b[l], w_down_b[l], norm_final[None],
                 tt=tok, tf=1024, final=final)

        u_s = jnp.concatenate([cache_conv[l].astype(F32), ps['u'].reshape(bs, ls, G)], axis=1)
        outs['conv_p'].append(pp['u'].reshape(bp, sp, G)[:, sp - HALO:])
        outs['conv_s'].append(u_s[:, ls:])
        for name in _PROJ_KV:
            outs[name + '_s'].append(ps[name])
        outs['flf_p'].append(flf_pt)
        outs['flf_s'].append(flf_s)
        outs['hg_p'].append(st_p)
        outs['hg_s'].append(st_s)
        outs['mk_p'].append(mk)
        outs['mv_p'].append(mv)

    st = lambda n: jnp.stack(outs[n])
    kv_p = [jnp.transpose(t.reshape(depth, bp, H, HD, sp), (0, 1, 4, 2, 3)) for t in bufs]
    kv_s = {n: st(n + '_s').reshape(depth, bs, ls, H, HD) for n in _PROJ_KV}
    mem_p = [st(n).reshape(depth, bp, N_MEM, MEM_HEADS, MEM_HD) for n in ('mk_p', 'mv_p')]
    hg_p = _unblock_diag_t(st('hg_p').reshape(depth * bp, G, G)).reshape(depth, bp, H, HD, HD)
    hg_s = _unblock_diag_t(st('hg_s').reshape(depth * bs, G, G)).reshape(depth, bs, H, HD, HD)
    return (xp.reshape(bp, sp, D_MODEL), xs.reshape(bs, ls, D_MODEL),
            st('conv_p'), kv_p[0], kv_p[1], jnp.swapaxes(st('flf_p'), 2, 3), hg_p,
            kv_p[2], kv_p[3], mem_p[0], mem_p[1],
            st('conv_s'), kv_s['fk'], kv_s['fv'], st('flf_s'), hg_s,
            kv_s['sk'], kv_s['sv'])
```

```python
import functools

import jax
import jax.numpy as jnp
from jax import lax
from jax.experimental import pallas as pl
from jax.experimental.pallas import tpu as pltpu

F32 = jnp.float32
BF16 = jnp.bfloat16

D_MODEL = 1024
G = 256
H = 4
HD = 64
CONV_W = 31
HALO = CONV_W - 1
N_MEM = 256
MEM_HEADS = 4
MEM_HD = 256
D_FF = 4096
EPS = 1e-6
NEG_BIG = -1e30
SUB = 8
CHUNK = 64
SUB_SHIFT = SUB.bit_length() - 1
CHUNK_SHIFT = CHUNK.bit_length() - 1
HD_SHIFT = HD.bit_length() - 1
FF_PAD = 128
LANES = 128
SB_BLOCK = 256
ATTN_GROUPS = 8
ATTN_GROUP_MIN_ROWS = 256
LOG2E = 1.4426950408889634
SOFTPLUS_LINEAR = 64.0
QK_SCALE = HD ** -0.5 * LOG2E
VMEM_LIMIT_BYTES = 52 * 1024 * 1024


def _params(*sem):
    return pltpu.CompilerParams(dimension_semantics=sem, vmem_limit_bytes=VMEM_LIMIT_BYTES)


def _logsig(x):
    return jnp.minimum(x, 0.0) - jnp.log(1.0 + jnp.exp(-jnp.abs(x)))


def _sigmoid(x):
    return 1.0 / (1.0 + jnp.exp(-x))


def _rms(x, g):
    return x * lax.rsqrt(jnp.mean(x * x, axis=-1, keepdims=True) + EPS) * g


def _dot(a, b):
    return jnp.dot(a, b, preferred_element_type=F32)


def _dot_nt(a, b):
    return lax.dot_general(a, b, (((1,), (1,)), ((), ())), preferred_element_type=F32)


def _dot_tn(a, b):
    return lax.dot_general(a, b, (((0,), (0,)), ((), ())), preferred_element_type=F32)


def _stack_heads(x, lane_head):
    return jnp.concatenate([jnp.where(lane_head == h, x, jnp.zeros_like(x)) for h in range(H)], axis=0)


def _unstack_heads(x4, lane_head, t):
    out = jnp.zeros((t, G), x4.dtype)
    for h in range(H):
        out = jnp.where(lane_head == h, x4[h * t:(h + 1) * t], out)
    return out


def _proj_groups(x_ref, g_ref, w_ref, bff_ref, hc_ref):
    h = _rms(x_ref[...], g_ref[...]).astype(BF16)

    def col(i, n=G):
        return _dot(h, w_ref[:, i * G:i * G + n])

    yield 'u', col(0) * _sigmoid(col(1))
    yield 'fq', col(2) * QK_SCALE
    yield 'fk', col(3)
    yield 'fv', col(4)
    yield 'hq', col(5)
    hz = col(6)
    base = hc_ref[0:1, :] + _logsig(hz)
    c1 = hc_ref[1:2, :]
    lae = jnp.maximum(c1, base) + jnp.log(1.0 + jnp.exp(-jnp.abs(c1 - base)))
    yield 'hlf', jnp.where(hc_ref[2:3, :] > 0.5, lae, base)
    yield 'hk', hc_ref[3:4, :] * _sigmoid(-hz)
    yield 'hv', col(7)
    yield 'hgate', col(8)
    yield 'sq', col(9) * QK_SCALE
    yield 'sk', col(10)
    yield 'sv', col(11)
    yield 'flf', _logsig(col(12, FF_PAD) + bff_ref[...])


_PROJ_ROW_F32 = ('u', 'hq', 'hlf', 'hk', 'hv', 'hgate')
_PROJ_ROW_BF16 = ('fq', 'sq')
_PROJ_KV = ('fk', 'fv', 'sk', 'sv')


def _proj_rows_kernel(x_ref, g_ref, w_ref, bff_ref, hc_ref, *out_refs):
    names = _PROJ_ROW_F32 + _PROJ_ROW_BF16 + ('flf',) + tuple(n + s for n in _PROJ_KV for s in ('', '_b'))
    refs = dict(zip(names, out_refs))
    for name, val in _proj_groups(x_ref, g_ref, w_ref, bff_ref, hc_ref):
        if name in _PROJ_KV:
            refs[name][...] = val
            refs[name + '_b'][...] = val.astype(BF16)
        else:
            refs[name][...] = val.astype(refs[name].dtype)


def _proj_rows(x, g, w, bff, hc, tt):
    t = x.shape[0]
    row = lambda i: (i, 0)
    fix = lambda i: (0, 0)
    f32o = jax.ShapeDtypeStruct((t, G), F32)
    b16o = jax.ShapeDtypeStruct((t, G), BF16)
    outs = ([f32o] * len(_PROJ_ROW_F32) + [b16o] * len(_PROJ_ROW_BF16)
            + [jax.ShapeDtypeStruct((t, FF_PAD), F32)] + [f32o, b16o] * len(_PROJ_KV))
    names = _PROJ_ROW_F32 + _PROJ_ROW_BF16 + ('flf',) + tuple(n + s for n in _PROJ_KV for s in ('', '_b'))
    res = pl.pallas_call(
        _proj_rows_kernel,
        grid=(t // tt,),
        in_specs=[pl.BlockSpec((tt, D_MODEL), row), pl.BlockSpec((1, D_MODEL), fix),
                  pl.BlockSpec((D_MODEL, w.shape[1]), fix), pl.BlockSpec((1, FF_PAD), fix),
                  pl.BlockSpec((4, G), fix)],
        out_specs=[pl.BlockSpec((tt, o.shape[1]), row) for o in outs],
        out_shape=outs,
        compiler_params=_params("parallel"),
        name="proj_rows",
    )(x, g, w, bff, hc)
    return dict(zip(names, res))


_SEQ_ROW_F32 = ('u',)
_SEQ_ROW_BF16 = ('fq', 'sq', 'conv_o', 'hg_o')
_SEQ_HGRN_IN = ('hq', 'hk', 'hv', 'hlf', 'hgate')
_SEQ_NAMES = (_SEQ_ROW_F32 + _SEQ_ROW_BF16 + ('flf',)
              + tuple(n + s for n in _PROJ_KV for s in ('_t', '_tb')) + ('st_out',))


def _proj_seq_kernel(x_ref, g_ref, w_ref, bff_ref, hc_ref, cw_ref, cp_ref, hn_ref, *rest, n_alias, nt, tt):
    rest = rest[n_alias:]
    refs = dict(zip(_SEQ_NAMES, rest[:len(_SEQ_NAMES)]))
    scratch = rest[len(_SEQ_NAMES):]
    i = pl.program_id(0) % nt
    _seq_start(i, scratch, tt)
    _seq_tile(x_ref, g_ref, w_ref, bff_ref, hc_ref, cw_ref, cp_ref, hn_ref, refs, scratch, tt)
    _seq_finish(i, nt, refs, scratch)


def _seq_start(i, scratch, tt):
    ext_ref, st_ref = scratch[:2]

    @pl.when(i == 0)
    def _():
        _conv_start(ext_ref, jnp.zeros((HALO, G), F32), tt)
        st_ref[...] = jnp.zeros(st_ref.shape, F32)


def _seq_finish(i, nt, refs, scratch):
    @pl.when(i == nt - 1)
    def _():
        refs['st_out'][0] = scratch[1][...]


def _seq_tile(x_ref, g_ref, w_ref, bff_ref, hc_ref, cw_ref, cp_ref, hn_ref, refs, scratch, tt,
              between=lambda: None):
    ext_ref, st_ref, kpad, vpad, bpad = scratch
    vals = {}
    for name, val in _proj_groups(x_ref, g_ref, w_ref, bff_ref, hc_ref):
        if name in _PROJ_KV:
            vt = val.T
            refs[name + '_t'][0, 0] = vt
            refs[name + '_tb'][0, 0] = vt.astype(BF16)
        elif name == 'flf':
            refs[name][0] = val.T[0:8, :]
        elif name in _SEQ_HGRN_IN:
            vals[name] = val
        else:
            refs[name][...] = val.astype(refs[name].dtype)
            if name == 'u':
                vals[name] = val
    between()
    _conv_tile(vals['u'], cw_ref, cp_ref, refs['conv_o'], ext_ref, tt)
    between()
    _hgrn_tile(vals['hq'], vals['hk'], vals['hv'], vals['hlf'], vals['hgate'], hn_ref,
               refs['hg_o'], st_ref, kpad, vpad, bpad, tt, between)


def _seq_outputs(t, depth, b, l, tt, tile_w, layer, tile_of):
    nt = l // tt
    per = tile_w // tt
    f32o = jax.ShapeDtypeStruct((t, G), F32)
    b16o = jax.ShapeDtypeStruct((t, G), BF16)
    stacked = jax.ShapeDtypeStruct((depth, b, G, l), F32)
    tiled = jax.ShapeDtypeStruct((b, l // tile_w, G, tile_w), BF16)
    n_rows = len(_SEQ_ROW_F32) + len(_SEQ_ROW_BF16)
    outs = ([f32o] * len(_SEQ_ROW_F32) + [b16o] * len(_SEQ_ROW_BF16)
            + [jax.ShapeDtypeStruct((b, 8, l), F32)] + [stacked, tiled] * len(_PROJ_KV)
            + [jax.ShapeDtypeStruct((b, G, G), F32)])
    seq = lambda i: tile_of(i) // nt
    pos = lambda i: tile_of(i) % nt
    ospecs = ([pl.BlockSpec((tt, G), lambda i: (tile_of(i), 0))] * n_rows
              + [pl.BlockSpec((1, 8, tt), lambda i: (seq(i), 0, pos(i)))]
              + [pl.BlockSpec((1, 1, G, tt), lambda i: (layer, seq(i), 0, pos(i))),
                 pl.BlockSpec((1, 1, G, tt), lambda i: (seq(i), pos(i) // per, 0, pos(i) % per))] * len(_PROJ_KV)
              + [pl.BlockSpec((1, G, G), lambda i: (seq(i), 0, 0))])
    return outs, ospecs, n_rows + 1


def _seq_scratch(tt):
    return ([pltpu.VMEM((tt + 40, G), F32), pltpu.VMEM((G, G), F32)]
            + [pltpu.VMEM((tt + SUB, G), F32)] * 3)


def _seq_in_specs(w, resident):
    fix = lambda i: (0, 0)
    spec = (lambda shape: pl.BlockSpec(shape, fix, pipeline_mode=pl.Buffered(1))) if resident else \
           (lambda shape: pl.BlockSpec(shape, fix))
    return [pl.BlockSpec((1, D_MODEL), fix), spec((D_MODEL, w.shape[1])), pl.BlockSpec((1, FF_PAD), fix),
            pl.BlockSpec((4, G), fix), pl.BlockSpec((CONV_W, G), fix), pl.BlockSpec((3, G), fix),
            pl.BlockSpec((1, G), fix)]


def _proj_seq(x, g, w, bff, hc, cw, cp, hn, bufs, *, layer, depth, b, l, tt):
    t = x.shape[0]
    outs, ospecs, first_stacked = _seq_outputs(t, depth, b, l, tt, tt, layer, lambda i: i)
    n_in = 8
    alias_args = list(bufs)
    aliases = {n_in + k: first_stacked + 2 * k for k in range(len(_PROJ_KV))}
    res = pl.pallas_call(
        functools.partial(_proj_seq_kernel, n_alias=len(alias_args), nt=l // tt, tt=tt),
        grid=(t // tt,),
        in_specs=[pl.BlockSpec((tt, D_MODEL), lambda i: (i, 0))] + _seq_in_specs(w, False)
                 + [pl.BlockSpec(memory_space=pl.ANY)] * len(alias_args),
        out_specs=ospecs,
        out_shape=outs,
        input_output_aliases=aliases,
        scratch_shapes=_seq_scratch(tt),
        compiler_params=_params("arbitrary"),
        name="proj_seq",
    )(x, g, w, bff, hc, cw, cp, hn, *alias_args)
    return dict(zip(_SEQ_NAMES, res))


def _mlp_proj_kernel(x1_ref, o_ref, wmo_ref, gf_ref, wup_ref, wdn_ref,
                     g_ref, w_ref, bff_ref, hc_ref, cw_ref, cp_ref, hn_ref, *rest, n_alias, nt, tt, tf):
    rest = rest[n_alias:]
    x3_ref = rest[0]
    refs = dict(zip(_SEQ_NAMES, rest[1:1 + len(_SEQ_NAMES)]))
    xprev_ref = rest[1 + len(_SEQ_NAMES)]
    scratch = rest[2 + len(_SEQ_NAMES):]
    step = pl.program_id(0)
    i = jnp.maximum(step - 1, 0) % nt

    @pl.when(step == 0)
    def _():
        xprev_ref[...] = jnp.zeros(xprev_ref.shape, F32)

    _seq_start(i, scratch, tt)
    mlp = {}
    pieces = _mlp_pieces(x1_ref, o_ref, wmo_ref, gf_ref, wup_ref, wdn_ref, tf, mlp)

    def between():
        if pieces:
            pieces.pop(0)()

    _seq_tile(xprev_ref, g_ref, w_ref, bff_ref, hc_ref, cw_ref, cp_ref, hn_ref, refs, scratch, tt, between)
    while pieces:
        between()
    x3_ref[...] = mlp['x3']
    xprev_ref[...] = mlp['x3']
    _seq_finish(i, nt, refs, scratch)


def _mlp_proj(x1, o, wmo, gf, wup, wdn, g, w, bff, hc, cw, cp, hn, bufs, *, layer, depth, b, l, tt, tile_w, tf):
    t = x1.shape[0]
    n_tiles = t // tt
    mlp_tile = lambda s: (jnp.minimum(s, n_tiles - 1), 0)
    fix = lambda s: (0, 0)
    resident = lambda shape: pl.BlockSpec(shape, fix, pipeline_mode=pl.Buffered(1))
    outs, ospecs, first_stacked = _seq_outputs(t, depth, b, l, tt, tile_w, layer, lambda s: jnp.maximum(s - 1, 0))
    n_in = 13
    alias_args = list(bufs)
    aliases = {n_in + k: 1 + first_stacked + 2 * k for k in range(len(_PROJ_KV))}
    res = pl.pallas_call(
        functools.partial(_mlp_proj_kernel, n_alias=len(alias_args), nt=l // tt, tt=tt, tf=tf),
        grid=(n_tiles + 1,),
        in_specs=[pl.BlockSpec((tt, D_MODEL), mlp_tile), pl.BlockSpec((tt, D_MODEL), mlp_tile),
                  resident((D_MODEL, D_MODEL)), pl.BlockSpec((1, D_MODEL), fix),
                  resident((D_MODEL, D_FF)), resident((D_FF, D_MODEL))] + _seq_in_specs(w, True)
                 + [pl.BlockSpec(memory_space=pl.ANY)] * len(alias_args),
        out_specs=[pl.BlockSpec((tt, D_MODEL), mlp_tile)] + ospecs,
        out_shape=[jax.ShapeDtypeStruct((t, D_MODEL), F32)] + outs,
        input_output_aliases=aliases,
        scratch_shapes=[pltpu.VMEM((tt, D_MODEL), F32)] + _seq_scratch(tt),
        compiler_params=_params("arbitrary"),
        name="mlp_proj",
    )(x1, o, wmo, gf, wup, wdn, g, w, bff, hc, cw, cp, hn, *alias_args)
    return res[0], dict(zip(_SEQ_NAMES, res[1:]))


def _cumsum_kernel(x_ref, o_ref, *, scale):
    x = x_ref[...]
    n = x.shape[1]
    lane = lax.broadcasted_iota(jnp.int32, x.shape, 1)
    sh = 1
    while sh < n:
        x = x + jnp.where(lane >= sh, pltpu.roll(x, sh, axis=1), 0.0)
        sh *= 2
    o_ref[...] = x * scale


def _cumsum_lanes(x, scale):
    return pl.pallas_call(
        functools.partial(_cumsum_kernel, scale=scale),
        out_shape=jax.ShapeDtypeStruct(x.shape, F32),
        compiler_params=pltpu.CompilerParams(vmem_limit_bytes=VMEM_LIMIT_BYTES),
        name="cumsum",
    )(x)


CONV_BASE = 32 - HALO


def _conv_start(ext_ref, hist, tt):
    ext_ref[0:CONV_BASE, :] = jnp.zeros((CONV_BASE, G), F32)
    ext_ref[CONV_BASE:32, :] = hist
    ext_ref[32 + tt:40 + tt, :] = jnp.zeros((8, G), F32)


def _conv_tile(u, w_ref, p_ref, o_ref, ext_ref, tt):
    base = CONV_BASE
    ext_ref[32:32 + tt, :] = u
    acc = None
    for r in range(8):
        part = None
        for m in range((base + CONV_W - 1) // 8 + 1):
            j = 8 * m + r - base
            if 0 <= j < CONV_W:
                term = w_ref[j:j + 1, :] * ext_ref[8 * m:8 * m + tt + 8, :]
                part = term if part is None else part + term
        part = part[r:r + tt, :]
        acc = part if acc is None else acc + part
    yf = acc + p_ref[0:1, :]
    mu = jnp.mean(yf, axis=-1, keepdims=True)
    d = yf - mu
    var = jnp.mean(d * d, axis=-1, keepdims=True)
    yn = d * lax.rsqrt(var + EPS) * p_ref[1:2, :] + p_ref[2:3, :]
    o_ref[...] = (yn * _sigmoid(yn)).astype(BF16)
    ext_ref[base:32, :] = ext_ref[tt + base:tt + 32, :]


def _conv_kernel(u_ref, hist_ref, w_ref, p_ref, o_ref, ext_ref, *, tt):
    @pl.when(pl.program_id(1) == 0)
    def _():
        _conv_start(ext_ref, hist_ref[0], tt)

    _conv_tile(u_ref[...], w_ref, p_ref, o_ref, ext_ref, tt)


def _conv(u, hist, w, p, *, row_off, b, l, tt):
    nt = l // tt
    off = row_off // tt
    return pl.pallas_call(
        functools.partial(_conv_kernel, tt=tt),
        grid=(b, nt),
        in_specs=[pl.BlockSpec((tt, G), lambda bi, i: (off + bi * nt + i, 0)),
                  pl.BlockSpec((1, HALO, G), lambda bi, i: (bi, 0, 0)),
                  pl.BlockSpec((CONV_W, G), lambda bi, i: (0, 0)),
                  pl.BlockSpec((3, G), lambda bi, i: (0, 0))],
        out_specs=pl.BlockSpec((tt, G), lambda bi, i: (bi * nt + i, 0)),
        out_shape=jax.ShapeDtypeStruct((b * l, G), BF16),
        scratch_shapes=[pltpu.VMEM((tt + 40, G), F32)],
        compiler_params=_params("arbitrary", "arbitrary"),
        name="conv",
    )(u, hist, w, p)


def _lanes(x, w):
    if w < LANES:
        return x[:, :w]
    return x if w == LANES else jnp.concatenate([x] * (w // LANES), axis=1)


def _scores(q4, k, t_layout):
    return _dot(q4, k) if t_layout else _dot_nt(q4, k)


def _weighted(p4, v, t_layout):
    return _dot_nt(p4, v) if t_layout else _dot(p4, v)


def _fox_kernel(*refs, tq, tkp, n_past, cur_t):
    if n_past:
        (q_ref, kc_ref, vc_ref, ckc_ref, cq_ref, kp_ref, vp_ref, ckp_ref,
         o_ref, m_ref, l_ref, acc_ref) = refs
    else:
        q_ref, kc_ref, vc_ref, ckc_ref, cq_ref, o_ref, m_ref, l_ref, acc_ref = refs
    i = pl.program_id(1)
    lane_head = lax.broadcasted_iota(jnp.int32, (tq, G), 1) >> HD_SHIFT
    q4 = _stack_heads(q_ref[...], lane_head)
    cq = cq_ref[...]
    cqb = [jnp.broadcast_to(cq[:, h:h + 1], (tq, LANES)) for h in range(H)]
    tri = (lax.broadcasted_iota(jnp.int32, (tq, tq), 1)
           <= lax.broadcasted_iota(jnp.int32, (tq, tq), 0))
    m_ref[...] = jnp.full(m_ref.shape, NEG_BIG, F32)
    l_ref[...] = jnp.zeros(l_ref.shape, F32)
    acc_ref[...] = jnp.zeros(acc_ref.shape, F32)

    def step(k, v, ck, masked, t_layout):
        s = _scores(q4, k, t_layout)
        w = s.shape[1]
        ps = []
        for h in range(H):
            sl = slice(h * tq, (h + 1) * tq)
            sh = s[sl] - ck[h:h + 1, :]
            if masked:
                sh = jnp.where(tri, sh, NEG_BIG)
            m_old = m_ref[sl]
            m_new = jnp.maximum(m_old, jnp.max(sh, axis=-1, keepdims=True) + cqb[h])
            p = jnp.exp2(sh - _lanes(m_new - cqb[h], w))
            alpha = jnp.exp2(m_old - m_new)
            l_ref[sl] = alpha * l_ref[sl] + jnp.sum(p, axis=-1, keepdims=True)
            m_ref[sl] = m_new
            acc_ref[sl] = acc_ref[sl] * _lanes(alpha, G)
            ps.append(p.astype(BF16))
        acc_ref[...] += _weighted(jnp.concatenate(ps, axis=0), v, t_layout)

    for j in range(n_past):
        sl = slice(j * tkp, (j + 1) * tkp)
        step(kp_ref[0, 0, :, sl].astype(BF16), vp_ref[0, 0, :, sl].astype(BF16), ckp_ref[0, j], False, True)

    if cur_t:
        def cur_body(j, c):
            step(kc_ref[0, j], vc_ref[0, j], ckc_ref[0, j], False, True)
            return c
        lax.fori_loop(0, i, cur_body, 0)
        step(kc_ref[0, i], vc_ref[0, i], ckc_ref[0, i], True, True)
    else:
        step(kc_ref[...], vc_ref[...], ckc_ref[0, 0], True, False)

    acc4 = jnp.concatenate(
        [acc_ref[h * tq:(h + 1) * tq] * _lanes(1.0 / l_ref[h * tq:(h + 1) * tq], G) for h in range(H)],
        axis=0)
    o_ref[...] = _unstack_heads(acc4, lane_head, tq).astype(BF16)


def _sb_kernel(*refs, tq, tkp, n_past, cur_t):
    if n_past:
        q_ref, kc_ref, vc_ref, kp_ref, vp_ref, o_ref, r_ref, acc_ref = refs
    else:
        q_ref, kc_ref, vc_ref, o_ref, r_ref, acc_ref = refs
    i = pl.program_id(1)
    lane_head = lax.broadcasted_iota(jnp.int32, (tq, G), 1) >> HD_SHIFT
    q4 = _stack_heads(q_ref[...], lane_head)
    r_ref[...] = jnp.zeros(r_ref.shape, F32)
    acc_ref[...] = jnp.zeros(acc_ref.shape, F32)

    def later_matrix(n):
        later = (lax.broadcasted_iota(jnp.int32, (n, n), 0) > lax.broadcasted_iota(jnp.int32, (n, n), 1))
        return jnp.where(later, 1.0, 0.0).astype(BF16)

    widths = {min(tq, SB_BLOCK)} | ({min(tkp, SB_BLOCK)} if n_past else set())
    later = {n: later_matrix(n) for n in widths}

    tri = (lax.broadcasted_iota(jnp.int32, (tq, tq), 1)
           < lax.broadcasted_iota(jnp.int32, (tq, tq), 0))
    valid = jnp.concatenate([tri] * H, axis=0)
    groups = ATTN_GROUPS if H * tq // ATTN_GROUPS >= ATTN_GROUP_MIN_ROWS else 1

    def step(k, v, masked, t_layout):
        for g in range(groups):
            rows = slice(g * (H * tq // groups), (g + 1) * (H * tq // groups))
            z = _scores(q4[rows], k, t_layout)
            w = z.shape[1]
            sp = jnp.where(z > SOFTPLUS_LINEAR, z, jnp.log2(1.0 + jnp.exp2(z)))
            u = jnp.where(valid[rows], sp, 0.0) if masked else sp
            bw = min(w, SB_BLOCK)
            r_run = r_ref[rows]
            rests = [None] * (w // bw)
            for blk in reversed(range(w // bw)):
                ub = u[:, blk * bw:(blk + 1) * bw]
                rests[blk] = _dot(ub.astype(BF16), later[bw]) + _lanes(r_run, bw)
                r_run = r_run + jnp.sum(ub, axis=-1, keepdims=True)
            rest = rests[0] if len(rests) == 1 else jnp.concatenate(rests, axis=1)
            a = jnp.exp2(z - sp - rest)
            if masked:
                a = jnp.where(valid[rows], a, 0.0)
            acc_ref[rows] += _weighted(a.astype(BF16), v, t_layout)
            r_ref[rows] = r_run

    if cur_t:
        step(kc_ref[0, i], vc_ref[0, i], True, True)

        def cur_body(n, c):
            j = i - 1 - n
            step(kc_ref[0, j], vc_ref[0, j], False, True)
            return c
        lax.fori_loop(0, i, cur_body, 0)
    else:
        step(kc_ref[...], vc_ref[...], True, False)

    for j in reversed(range(n_past)):
        sl = slice(j * tkp, (j + 1) * tkp)
        step(kp_ref[0, 0, :, sl].astype(BF16), vp_ref[0, 0, :, sl].astype(BF16), False, True)

    o_ref[...] = _unstack_heads(acc_ref[...], lane_head, tq).astype(BF16)


def _attention(kind, q, kc, vc, extra, past, *, layer, b, l, tq, tkp):
    nq = l // tq
    cur_t = kc.ndim == 4
    assert cur_t or nq == 1
    if cur_t:
        kv_spec = pl.BlockSpec((1, nq, G, tq), lambda bi, i: (bi, 0, 0, 0))
    else:
        kv_spec = pl.BlockSpec((l, G), lambda bi, i: (bi, 0))
    in_specs = [pl.BlockSpec((tq, G), lambda bi, i: (bi * nq + i, 0)), kv_spec, kv_spec]
    args = [q, kc, vc]
    if kind == 'fox':
        ckc, cq = extra
        in_specs += [pl.BlockSpec((1, nq, H, tq), lambda bi, i: (bi, 0, 0, 0)),
                     pl.BlockSpec((tq, H), lambda bi, i: (bi * nq + i, 0))]
        args += [ckc, cq]
    n_past = 0
    if past is not None:
        plen = past[0].shape[3]
        n_past = plen // tkp
        cache_spec = pl.BlockSpec((1, 1, G, plen), lambda bi, i: (layer, bi, 0, 0))
        in_specs += [cache_spec, cache_spec]
        args += [past[0], past[1]]
        if kind == 'fox':
            in_specs.append(pl.BlockSpec((1, n_past, H, tkp), lambda bi, i: (bi, 0, 0, 0)))
            args.append(past[2])
    stat = pltpu.VMEM((H * tq, LANES), F32)
    acc = pltpu.VMEM((H * tq, G), F32)
    body = _fox_kernel if kind == 'fox' else _sb_kernel
    return pl.pallas_call(
        functools.partial(body, tq=tq, tkp=tkp, n_past=n_past, cur_t=cur_t),
        grid=(b, nq),
        in_specs=in_specs,
        out_specs=pl.BlockSpec((tq, G), lambda bi, i: (bi * nq + i, 0)),
        out_shape=jax.ShapeDtypeStruct((b * l, G), BF16),
        scratch_shapes=[stat, stat, acc] if kind == 'fox' else [stat, acc],
        compiler_params=_params("parallel", "arbitrary"),
        name=kind,
    )(*args)


def _hgrn_tile(q, kk, v, lf, gate, hn_ref, o_ref, st_ref, kpad, vpad, bpad, tt, between=lambda: None):
    nc = tt // CHUNK
    row = lax.broadcasted_iota(jnp.int32, (tt, G), 0)
    r64 = row & (CHUNK - 1)
    rsub = row & (SUB - 1)
    lane_head = lax.broadcasted_iota(jnp.int32, (tt, G), 1) >> HD_SHIFT
    same_head = (lax.broadcasted_iota(jnp.int32, (G, G), 0) >> HD_SHIFT
                 == lax.broadcasted_iota(jnp.int32, (G, G), 1) >> HD_SHIFT)
    bd = jnp.where(same_head, 1.0, 0.0).astype(BF16)

    b = lf * LOG2E
    sh = 1
    while sh < CHUNK:
        b = b + jnp.where(r64 >= sh, pltpu.roll(b, sh, axis=0), 0.0)
        sh *= 2

    def chunk_row(r):
        return jnp.concatenate(
            [jnp.broadcast_to(b[c * CHUNK + r:c * CHUNK + r + 1, :], (CHUNK, G)) for c in range(nc)], axis=0)

    zpad = jnp.zeros((SUB, G), F32)
    kpad[0:SUB, :] = zpad
    vpad[0:SUB, :] = zpad
    bpad[0:SUB, :] = zpad
    kpad[SUB:SUB + tt, :] = kk
    vpad[SUB:SUB + tt, :] = v
    bpad[SUB:SUB + tt, :] = b
    o = jnp.zeros((tt, G), F32)
    for d in range(SUB):
        ks = kpad[SUB - d:SUB - d + tt, :]
        vs = vpad[SUB - d:SUB - d + tt, :]
        bs = bpad[SUB - d:SUB - d + tt, :]
        p = q * ks * jnp.exp2(jnp.minimum(b - bs, 0.0))
        p = jnp.where(rsub >= d, p, 0.0)
        o = o + _dot(p.astype(BF16), bd) * vs
        if d == SUB // 2 - 1:
            between()
    between()

    nsub = CHUNK // SUB
    refs_b = [chunk_row(SUB * n - 1) for n in range(1, nsub)]
    sub = r64 >> SUB_SHIFT
    rq = b
    for n, rb in enumerate(refs_b, start=1):
        rq = jnp.where(sub == n, rb, rq)
    qt = q * jnp.exp2(jnp.minimum(b - rq, 0.0))
    gr = SUB * nc
    lane_head_g = lax.broadcasted_iota(jnp.int32, (gr, G), 1) >> HD_SHIFT
    g_row = lax.broadcasted_iota(jnp.int32, (gr, tt), 0)
    g_col = lax.broadcasted_iota(jnp.int32, (gr, tt), 1)
    key_sub = jnp.where(g_row >> SUB_SHIFT == g_col >> CHUNK_SHIFT, (g_col & (CHUNK - 1)) >> SUB_SHIFT, nsub)
    key_sub4 = jnp.concatenate([key_sub] * H, axis=0)
    a_all = []
    for n, rb in enumerate(refs_b, start=1):
        qn = jnp.concatenate([qt[c * CHUNK + SUB * n:c * CHUNK + SUB * (n + 1)] for c in range(nc)], axis=0)
        q4n = _stack_heads(qn, lane_head_g).astype(BF16)
        kt = (kk * jnp.exp2(jnp.minimum(rb - b, 0.0))).astype(BF16)
        a_all.append(jnp.where(key_sub4 < n, _dot_nt(q4n, kt), 0.0).astype(BF16))
    vb = v.astype(BF16)
    o_all = _dot(jnp.concatenate(a_all, axis=0), vb)
    o_sub = [_unstack_heads(o_all[(n - 1) * H * gr:n * H * gr], lane_head_g, gr) for n in range(1, nsub)]
    pieces = []
    for c in range(nc):
        pieces.append(jnp.zeros((SUB, G), F32))
        pieces += [o_sub[n - 1][c * SUB:(c + 1) * SUB] for n in range(1, nsub)]
    o = o + jnp.concatenate(pieces, axis=0)

    between()

    blast = chunk_row(CHUNK - 1)
    qs = (q * jnp.exp2(b)).astype(BF16)
    kd = (kk * jnp.exp2(jnp.minimum(blast - b, 0.0))).astype(BF16)
    dec = jnp.exp2(blast)
    outs = []
    for c in range(nc):
        sl = slice(c * CHUNK, (c + 1) * CHUNK)
        st = st_ref[...]
        outs.append(_dot_nt(qs[sl], st.astype(BF16)))
        upd = _dot_tn(vb[sl], kd[sl])
        st_ref[...] = st * dec[c * CHUNK:c * CHUNK + 1, :] + jnp.where(same_head, upd, 0.0)
    o = o + jnp.concatenate(outs, axis=0)

    sq = o * o
    hi = sq.astype(BF16)
    lo = (sq - hi.astype(F32)).astype(BF16)
    ms = (_dot(hi, bd) + _dot(lo, bd)) * (1.0 / HD)
    o_ref[...] = (o * lax.rsqrt(ms + EPS) * hn_ref[...] * (gate * _sigmoid(gate))).astype(BF16)


def _hgrn_kernel(*refs, tt, has_init):
    if has_init:
        (q_ref, k_ref, v_ref, lf_ref, gate_ref, hn_ref, st0_ref,
         o_ref, sto_ref, st_ref, kpad, vpad, bpad) = refs
    else:
        (q_ref, k_ref, v_ref, lf_ref, gate_ref, hn_ref,
         o_ref, sto_ref, st_ref, kpad, vpad, bpad) = refs
    i = pl.program_id(1)

    @pl.when(i == 0)
    def _():
        if has_init:
            st_ref[...] = st0_ref[0]
        else:
            st_ref[...] = jnp.zeros(st_ref.shape, F32)

    _hgrn_tile(q_ref[...], k_ref[...], v_ref[...], lf_ref[...], gate_ref[...], hn_ref,
               o_ref, st_ref, kpad, vpad, bpad, tt)

    @pl.when(i == pl.num_programs(1) - 1)
    def _():
        sto_ref[0] = st_ref[...]


def _hgrn(q, k, v, lf, gate, hn, st0, *, row_off, b, l, tt):
    nt = l // tt
    off = row_off // tt
    tok = pl.BlockSpec((tt, G), lambda bi, i: (off + bi * nt + i, 0))
    in_specs = [tok, tok, tok, tok, tok, pl.BlockSpec((1, G), lambda bi, i: (0, 0))]
    args = [q, k, v, lf, gate, hn]
    if st0 is not None:
        in_specs.append(pl.BlockSpec((1, G, G), lambda bi, i: (bi, 0, 0)))
        args.append(st0)
    return pl.pallas_call(
        functools.partial(_hgrn_kernel, tt=tt, has_init=st0 is not None),
        grid=(b, nt),
        in_specs=in_specs,
        out_specs=[pl.BlockSpec((tt, G), lambda bi, i: (bi * nt + i, 0)),
                   pl.BlockSpec((1, G, G), lambda bi, i: (bi, 0, 0))],
        out_shape=[jax.ShapeDtypeStruct((b * l, G), BF16), jax.ShapeDtypeStruct((b, G, G), F32)],
        scratch_shapes=[pltpu.VMEM((G, G), F32)] + [pltpu.VMEM((tt + SUB, G), F32)] * 3,
        compiler_params=_params("arbitrary", "arbitrary"),
        name="hgrn",
    )(*args)


def _c1_kernel(x_ref, m0_ref, m1_ref, m2_ref, m3_ref, wo_ref, g_ref, wq_ref, x1_ref, q_ref):
    acc = x_ref[...]
    for p, m_ref in enumerate((m0_ref, m1_ref, m2_ref, m3_ref)):
        acc = acc + _dot(m_ref[...], wo_ref[p * G:(p + 1) * G, :])
    x1_ref[...] = acc
    h = _rms(acc, g_ref[...]).astype(BF16)
    q_ref[...] = (_dot(h, wq_ref[...]) * (MEM_HD ** -0.5)).astype(BF16)


def _c1(x, mix, wo, g, wq, tt):
    t = x.shape[0]
    row = lambda i: (i, 0)
    fix = lambda i: (0, 0)
    return pl.pallas_call(
        _c1_kernel,
        grid=(t // tt,),
        in_specs=[pl.BlockSpec((tt, D_MODEL), row)] + [pl.BlockSpec((tt, G), row)] * 4
                 + [pl.BlockSpec((D_MODEL, D_MODEL), fix), pl.BlockSpec((1, D_MODEL), fix),
                    pl.BlockSpec((D_MODEL, D_MODEL), fix)],
        out_specs=[pl.BlockSpec((tt, D_MODEL), row), pl.BlockSpec((tt, D_MODEL), row)],
        out_shape=[jax.ShapeDtypeStruct((t, D_MODEL), F32), jax.ShapeDtypeStruct((t, D_MODEL), BF16)],
        compiler_params=_params("parallel"),
        name="outproj_memq",
    )(x, *mix, wo, g, wq)


def _memkv_kernel(m_ref, wk_ref, wv_ref, k_ref, kb_ref, v_ref, vb_ref):
    m = m_ref[...].astype(BF16)
    k = _dot(m, wk_ref[...])
    k_ref[...] = k
    kb_ref[...] = k.astype(BF16)
    v = _dot(m, wv_ref[...])
    v_ref[...] = v
    vb_ref[...] = v.astype(BF16)


def _memkv(mem, wk, wv, tt):
    t = mem.shape[0]
    row = lambda i: (i, 0)
    fix = lambda i: (0, 0)
    f32o = jax.ShapeDtypeStruct((t, D_MODEL), F32)
    b16o = jax.ShapeDtypeStruct((t, D_MODEL), BF16)
    return pl.pallas_call(
        _memkv_kernel,
        grid=(t // tt,),
        in_specs=[pl.BlockSpec((tt, D_MODEL), row), pl.BlockSpec((D_MODEL, D_MODEL), fix),
                  pl.BlockSpec((D_MODEL, D_MODEL), fix)],
        out_specs=[pl.BlockSpec((tt, D_MODEL), row)] * 4,
        out_shape=[f32o, b16o, f32o, b16o],
        compiler_params=_params("parallel"),
        name="memkv",
    )(mem, wk, wv)


def _memattn_kernel(q_ref, k_ref, v_ref, o_ref):
    outs = []
    for h in range(MEM_HEADS):
        sl = slice(h * MEM_HD, (h + 1) * MEM_HD)
        s = _dot_nt(q_ref[:, sl], k_ref[:, sl].astype(BF16))
        p = jnp.exp(s - jnp.max(s, axis=-1, keepdims=True))
        den = jnp.sum(p, axis=-1, keepdims=True)
        outs.append(_dot(p.astype(BF16), v_ref[:, sl].astype(BF16)) * (1.0 / den))
    o_ref[...] = jnp.concatenate(outs, axis=-1).astype(BF16)


def _memattn(q, k, v, *, row_off, b, l, tq):
    nq = l // tq
    off = row_off // tq
    return pl.pallas_call(
        _memattn_kernel,
        grid=(b, nq),
        in_specs=[pl.BlockSpec((tq, D_MODEL), lambda bi, i: (off + bi * nq + i, 0)),
                  pl.BlockSpec((N_MEM, D_MODEL), lambda bi, i: (bi, 0)),
                  pl.BlockSpec((N_MEM, D_MODEL), lambda bi, i: (bi, 0))],
        out_specs=pl.BlockSpec((tq, D_MODEL), lambda bi, i: (bi * nq + i, 0)),
        out_shape=jax.ShapeDtypeStruct((b * l, D_MODEL), BF16),
        compiler_params=_params("parallel", "parallel"),
        name="memattn",
    )(q, k, v)


def _mlp_pieces(x1_ref, o_ref, wmo_ref, g_ref, wup_ref, wdn_ref, tf, out):
    def first():
        x2 = x1_ref[...] + _dot(o_ref[...], wmo_ref[...])
        out['h'] = _rms(x2, g_ref[...]).astype(BF16)
        out['x3'] = x2

    def chunk(f):
        a = jnp.maximum(_dot(out['h'], wup_ref[:, f * tf:(f + 1) * tf]), 0.0)
        out['x3'] = out['x3'] + _dot((a * a).astype(BF16), wdn_ref[f * tf:(f + 1) * tf, :])

    return [first] + [functools.partial(chunk, f) for f in range(D_FF // tf)]


def _mlp_tile(x1_ref, o_ref, wmo_ref, g_ref, wup_ref, wdn_ref, tf):
    out = {}
    for piece in _mlp_pieces(x1_ref, o_ref, wmo_ref, g_ref, wup_ref, wdn_ref, tf, out):
        piece()
    return out['x3']


def _c3_kernel(x1_ref, o_ref, wmo_ref, g_ref, wup_ref, wdn_ref, gf_ref, y_ref, *, final, tf):
    x3 = _mlp_tile(x1_ref, o_ref, wmo_ref, g_ref, wup_ref, wdn_ref, tf)
    y_ref[...] = _rms(x3, gf_ref[...]) if final else x3


def _c3(x1, o, wmo, g, wup, wdn, gf, *, tt, tf, final):
    t = x1.shape[0]
    row = lambda i: (i, 0)
    fix = lambda i: (0, 0)
    resident = lambda shape: pl.BlockSpec(shape, fix, pipeline_mode=pl.Buffered(1))
    return pl.pallas_call(
        functools.partial(_c3_kernel, final=final, tf=tf),
        grid=(t // tt,),
        in_specs=[pl.BlockSpec((tt, D_MODEL), row), pl.BlockSpec((tt, D_MODEL), row),
                  resident((D_MODEL, D_MODEL)), pl.BlockSpec((1, D_MODEL), fix),
                  resident((D_MODEL, D_FF)), resident((D_FF, D_MODEL)),
                  pl.BlockSpec((1, D_MODEL), fix)],
        out_specs=pl.BlockSpec((tt, D_MODEL), row),
        out_shape=jax.ShapeDtypeStruct((t, D_MODEL), F32),
        compiler_params=_params("parallel"),
        name="memout_mlp",
    )(x1, o, wmo, g, wup, wdn, gf)


def _block_diag_t(s):
    b = s.shape[0]
    eye = jnp.eye(H, dtype=s.dtype)
    st = jnp.swapaxes(s, 2, 3)
    return jnp.einsum('bhvk,hg->bhvgk', st, eye).reshape(b, G, G)


def _unblock_diag_t(st):
    b = st.shape[0]
    s5 = st.reshape(b, H, HD, H, HD)
    d = jnp.stack([s5[:, h, :, h, :] for h in range(H)], axis=1)
    return jnp.swapaxes(d, 2, 3)


def _row_cumsum(lf, tile, scale):
    b, s, _ = lf.shape
    sp = -(-s // tile) * tile
    x = jnp.swapaxes(lf, 1, 2).reshape(b * H, s)
    x = jnp.pad(x, ((0, 0), (0, sp - s)))
    return _cumsum_lanes(x, scale).reshape(b, H, sp)[:, :, :s]


def _tiles(c, n, t):
    b = c.shape[0]
    return jnp.swapaxes(c.reshape(b, H, n, t), 1, 2)


def kernel(x_prompt, x_sample, mem_prompt, cache_conv, cache_fox_k, cache_fox_v, cache_fox_logf,
           state_hgrn, cache_sb_k, cache_sb_v, cache_mem_k, cache_mem_v, norm_mix, w_in, b_fox_f,
           conv_w, conv_b, conv_ln_g, conv_ln_b, hgrn_lb, hgrn_norm, w_out, norm_mem, w_mq, w_mk,
           w_mv, w_mo, norm_ffn, w_up, w_down, norm_final):
    bp, sp, _ = x_prompt.shape
    bs, ls, _ = x_sample.shape
    depth = w_in.shape[0]
    past = cache_fox_k.shape[2]
    tp = bp * sp
    ts = bs * ls
    tok = 512
    tq_p = 512
    tk_past = 512
    tt_fused = 256

    sm = jax.nn.softmax(hgrn_lb.astype(F32), axis=0)
    lower = jnp.clip(jnp.cumsum(sm, axis=0) - sm[0], 0.0, 1.0 - 1e-6)
    pos = lower > 0.0
    hconst = jnp.stack([jnp.log1p(-lower), jnp.log(jnp.where(pos, lower, 1.0)),
                        pos.astype(F32), 1.0 - lower], axis=1)

    o_ff = 5 * G
    w_in_r = jnp.concatenate(
        [w_in[:, :, :o_ff], w_in[:, :, o_ff + H:],
         jnp.pad(w_in[:, :, o_ff:o_ff + H], ((0, 0), (0, 0), (0, FF_PAD - H)))], axis=-1).astype(BF16)
    bff = jnp.pad(b_fox_f, ((0, 0), (0, FF_PAD - H)))[:, None, :]
    w_out_b = w_out.astype(BF16)
    w_mq_b = w_mq.astype(BF16)
    w_mk_b = w_mk.astype(BF16)
    w_mv_b = w_mv.astype(BF16)
    w_mo_b = w_mo.astype(BF16)
    w_up_b = w_up.astype(BF16)
    w_down_b = w_down.astype(BF16)
    hn = jnp.tile(hgrn_norm, (1, H))[:, None, :]
    conv_p3 = jnp.stack([conv_b, conv_ln_g, conv_ln_b], axis=1)

    def seq_last(c):
        return jnp.transpose(c, (0, 1, 3, 4, 2)).reshape(depth, bs, G, past)

    fkc, fvc, skc, svc = (seq_last(c) for c in (cache_fox_k, cache_fox_v, cache_sb_k, cache_sb_v))

    st0_all = _block_diag_t(state_hgrn.astype(F32).reshape(depth * bs, H, HD, HD)).reshape(depth, bs, G, G)
    xp = x_prompt.reshape(tp, D_MODEL)
    xs = x_sample.reshape(ts, D_MODEL)
    mem2 = mem_prompt.reshape(bp * N_MEM, D_MODEL)
    bufs = tuple(jnp.zeros((depth, bp, G, sp), F32) for _ in _PROJ_KV)

    outs = {n: [] for n in ('conv_p', 'flf_p', 'hg_p', 'mk_p', 'mv_p',
                            'conv_s', 'fk_s', 'fv_s', 'flf_s', 'hg_s', 'sk_s', 'sv_s')}
    def seq_params(l):
        return (norm_mix[l][None], w_in_r[l], bff[l], hconst[l], conv_w[l], conv_p3[l], hn[l])

    pp = _proj_seq(xp, *seq_params(0), bufs, layer=0, depth=depth, b=bp, l=sp, tt=tq_p)
    for l in range(depth):
        bufs = tuple(pp[n + '_t'] for n in _PROJ_KV)
        conv_op, hg_op, st_p = pp['conv_o'], pp['hg_o'], pp['st_out']
        u_p = pp['u']
        ps = _proj_rows(xs, norm_mix[l][None], w_in_r[l], bff[l], hconst[l], tok)

        conv_os = _conv(ps['u'], cache_conv[l], conv_w[l], conv_p3[l], row_off=0, b=bs, l=ls, tt=ls)

        flf_pt = pp['flf'][:, :H, :]
        flf_s = ps['flf'][:, :H].reshape(bs, ls, H)
        c_p = _cumsum_lanes(flf_pt.reshape(bp * H, sp), LOG2E).reshape(bp, H, sp)
        c_s = _row_cumsum(jnp.concatenate([cache_fox_logf[l].astype(F32), flf_s], axis=1), LANES, LOG2E)
        fox_op = _attention('fox', pp['fq'], pp['fk_tb'], pp['fv_tb'],
                            (_tiles(c_p, sp // tq_p, tq_p), jnp.swapaxes(c_p, 1, 2).reshape(tp, H)),
                            None, layer=l, b=bp, l=sp, tq=tq_p, tkp=tk_past)
        fox_os = _attention('fox', ps['fq'], ps['fk_b'], ps['fv_b'],
                            (_tiles(c_s[:, :, past:], 1, ls), jnp.swapaxes(c_s[:, :, past:], 1, 2).reshape(ts, H)),
                            (fkc, fvc, _tiles(c_s[:, :, :past], past // tk_past, tk_past)),
                            layer=l, b=bs, l=ls, tq=ls, tkp=tk_past)

        hg_os, st_s = _hgrn(ps['hq'], ps['hk'], ps['hv'], ps['hlf'], ps['hgate'], hn[l],
                            st0_all[l], row_off=0, b=bs, l=ls, tt=ls)

        sb_op = _attention('sb', pp['sq'], pp['sk_tb'], pp['sv_tb'], (), None,
                           layer=l, b=bp, l=sp, tq=tq_p, tkp=tk_past)
        sb_os = _attention('sb', ps['sq'], ps['sk_b'], ps['sv_b'], (), (skc, svc),
                           layer=l, b=bs, l=ls, tq=ls, tkp=tk_past)

        x1p, qmp = _c1(xp, [conv_op, fox_op, hg_op, sb_op], w_out_b[l], norm_mem[l][None], w_mq_b[l], tok)
        x1s, qms = _c1(xs, [conv_os, fox_os, hg_os, sb_os], w_out_b[l], norm_mem[l][None], w_mq_b[l], tok)

        mk, mkb, mv, mvb = _memkv(mem2, w_mk_b[l], w_mv_b[l], N_MEM)
        om_p = _memattn(qmp, mkb, mvb, row_off=0, b=bp, l=sp, tq=tok)
        om_s = _memattn(qms, cache_mem_k[l].reshape(bs * N_MEM, D_MODEL),
                        cache_mem_v[l].reshape(bs * N_MEM, D_MODEL), row_off=0, b=bs, l=ls, tq=ls)

        final = l == depth - 1
        mlp_w = (w_mo_b[l], norm_ffn[l][None], w_up_b[l], w_down_b[l])
        if final:
            xp = _c3(x1p, om_p, *mlp_w, norm_final[None], tt=tok, tf=1024, final=True)
        else:
            xp, pp = _mlp_proj(x1p, om_p, *mlp_w, *seq_params(l + 1), bufs, layer=l + 1, depth=depth,
                               b=bp, l=sp, tt=tt_fused, tile_w=tq_p, tf=1024)
        xs = _c3(x1s, om_s, w_mo_b[l], norm_ffn[l][None], w_up_b[l], w_down_b[l], norm_final[None],
                 tt=tok, tf=1024, final=final)

        u_s = jnp.concatenate([cache_conv[l].astype(F32), ps['u'].reshape(bs, ls, G)], axis=1)
        outs['conv_p'].append(u_p.reshape(bp, sp, G)[:, sp - HALO:])
        outs['conv_s'].append(u_s[:, ls:])
        for name in _PROJ_KV:
            outs[name + '_s'].append(ps[name])
        outs['flf_p'].append(flf_pt)
        outs['flf_s'].append(flf_s)
        outs['hg_p'].append(st_p)
        outs['hg_s'].append(st_s)
        outs['mk_p'].append(mk)
        outs['mv_p'].append(mv)

    st = lambda n: jnp.stack(outs[n])
    kv_p = [jnp.transpose(t.reshape(depth, bp, H, HD, sp), (0, 1, 4, 2, 3)) for t in bufs]
    kv_s = {n: st(n + '_s').reshape(depth, bs, ls, H, HD) for n in _PROJ_KV}
    mem_p = [st(n).reshape(depth, bp, N_MEM, MEM_HEADS, MEM_HD) for n in ('mk_p', 'mv_p')]
    hg_p = _unblock_diag_t(st('hg_p').reshape(depth * bp, G, G)).reshape(depth, bp, H, HD, HD)
    hg_s = _unblock_diag_t(st('hg_s').reshape(depth * bs, G, G)).reshape(depth, bs, H, HD, HD)
    return (xp.reshape(bp, sp, D_MODEL), xs.reshape(bs, ls, D_MODEL),
            st('conv_p'), kv_p[0], kv_p[1], jnp.swapaxes(st('flf_p'), 2, 3), hg_p,
            kv_p[2], kv_p[3], mem_p[0], mem_p[1],
            st('conv_s'), kv_s['fk'], kv_s['fv'], st('flf_s'), hg_s,
            kv_s['sk'], kv_s['sv'])
```

```python
import functools

import jax
import jax.numpy as jnp
from jax import lax
from jax.experimental import pallas as pl
from jax.experimental.pallas import tpu as pltpu

F32 = jnp.float32
BF16 = jnp.bfloat16

D_MODEL = 1024
G = 256
H = 4
HD = 64
CONV_W = 31
HALO = CONV_W - 1
N_MEM = 256
MEM_HEADS = 4
MEM_HD = 256
D_FF = 4096
EPS = 1e-6
NEG_BIG = -1e30
SUB = 8
CHUNK = 64
SUB_SHIFT = SUB.bit_length() - 1
CHUNK_SHIFT = CHUNK.bit_length() - 1
HD_SHIFT = HD.bit_length() - 1
FF_PAD = 128
LANES = 128
SB_BLOCK = 256
ATTN_GROUPS = 8
ATTN_GROUP_MIN_ROWS = 256
LOG2E = 1.4426950408889634
SOFTPLUS_LINEAR = 64.0
QK_SCALE = HD ** -0.5 * LOG2E
VMEM_LIMIT_BYTES = 52 * 1024 * 1024


def _params(*sem):
    return pltpu.CompilerParams(dimension_semantics=sem, vmem_limit_bytes=VMEM_LIMIT_BYTES)


def _logsig(x):
    return jnp.minimum(x, 0.0) - jnp.log(1.0 + jnp.exp(-jnp.abs(x)))


def _sigmoid(x):
    return 1.0 / (1.0 + jnp.exp(-x))


def _rms(x, g):
    return x * lax.rsqrt(jnp.mean(x * x, axis=-1, keepdims=True) + EPS) * g


def _dot(a, b):
    return jnp.dot(a, b, preferred_element_type=F32)


def _dot_nt(a, b):
    return lax.dot_general(a, b, (((1,), (1,)), ((), ())), preferred_element_type=F32)


def _dot_tn(a, b):
    return lax.dot_general(a, b, (((0,), (0,)), ((), ())), preferred_element_type=F32)


def _stack_heads(x, lane_head):
    return jnp.concatenate([jnp.where(lane_head == h, x, jnp.zeros_like(x)) for h in range(H)], axis=0)


def _unstack_heads(x4, lane_head, t):
    out = jnp.zeros((t, G), x4.dtype)
    for h in range(H):
        out = jnp.where(lane_head == h, x4[h * t:(h + 1) * t], out)
    return out


def _proj_groups(x_ref, g_ref, w_ref, bff_ref, hc_ref):
    h = _rms(x_ref[...], g_ref[...]).astype(BF16)

    def col(i, n=G):
        return _dot(h, w_ref[:, i * G:i * G + n])

    yield 'u', col(0) * _sigmoid(col(1))
    yield 'fq', col(2) * QK_SCALE
    yield 'fk', col(3)
    yield 'fv', col(4)
    yield 'hq', col(5)
    hz = col(6)
    base = hc_ref[0:1, :] + _logsig(hz)
    c1 = hc_ref[1:2, :]
    lae = jnp.maximum(c1, base) + jnp.log(1.0 + jnp.exp(-jnp.abs(c1 - base)))
    yield 'hlf', jnp.where(hc_ref[2:3, :] > 0.5, lae, base)
    yield 'hk', hc_ref[3:4, :] * _sigmoid(-hz)
    yield 'hv', col(7)
    yield 'hgate', col(8)
    yield 'sq', col(9) * QK_SCALE
    yield 'sk', col(10)
    yield 'sv', col(11)
    yield 'flf', _logsig(col(12, FF_PAD) + bff_ref[...])


_PROJ_ROW_F32 = ('u', 'hq', 'hlf', 'hk', 'hv', 'hgate')
_PROJ_ROW_BF16 = ('fq', 'sq')
_PROJ_KV = ('fk', 'fv', 'sk', 'sv')


def _proj_rows_kernel(x_ref, g_ref, w_ref, bff_ref, hc_ref, *out_refs):
    names = _PROJ_ROW_F32 + _PROJ_ROW_BF16 + ('flf',) + tuple(n + s for n in _PROJ_KV for s in ('', '_b'))
    refs = dict(zip(names, out_refs))
    for name, val in _proj_groups(x_ref, g_ref, w_ref, bff_ref, hc_ref):
        if name in _PROJ_KV:
            refs[name][...] = val
            refs[name + '_b'][...] = val.astype(BF16)
        else:
            refs[name][...] = val.astype(refs[name].dtype)


def _proj_rows(x, g, w, bff, hc, tt):
    t = x.shape[0]
    row = lambda i: (i, 0)
    fix = lambda i: (0, 0)
    f32o = jax.ShapeDtypeStruct((t, G), F32)
    b16o = jax.ShapeDtypeStruct((t, G), BF16)
    outs = ([f32o] * len(_PROJ_ROW_F32) + [b16o] * len(_PROJ_ROW_BF16)
            + [jax.ShapeDtypeStruct((t, FF_PAD), F32)] + [f32o, b16o] * len(_PROJ_KV))
    names = _PROJ_ROW_F32 + _PROJ_ROW_BF16 + ('flf',) + tuple(n + s for n in _PROJ_KV for s in ('', '_b'))
    res = pl.pallas_call(
        _proj_rows_kernel,
        grid=(t // tt,),
        in_specs=[pl.BlockSpec((tt, D_MODEL), row), pl.BlockSpec((1, D_MODEL), fix),
                  pl.BlockSpec((D_MODEL, w.shape[1]), fix), pl.BlockSpec((1, FF_PAD), fix),
                  pl.BlockSpec((4, G), fix)],
        out_specs=[pl.BlockSpec((tt, o.shape[1]), row) for o in outs],
        out_shape=outs,
        compiler_params=_params("parallel"),
        name="proj_rows",
    )(x, g, w, bff, hc)
    return dict(zip(names, res))


_SEQ_ROW_F32 = ('u',)
_SEQ_ROW_BF16 = ('fq', 'sq', 'conv_o', 'hg_o')
_SEQ_HGRN_IN = ('hq', 'hk', 'hv', 'hlf', 'hgate')
_SEQ_NAMES = (_SEQ_ROW_F32 + _SEQ_ROW_BF16 + ('flf',)
              + tuple(n + s for n in _PROJ_KV for s in ('_t', '_tb')) + ('st_out',))


def _proj_seq_kernel(x_ref, g_ref, w_ref, bff_ref, hc_ref, cw_ref, cp_ref, hn_ref, *rest, nt, tt, depth):
    refs = dict(zip(_SEQ_NAMES, rest[:len(_SEQ_NAMES)]))
    scratch = rest[len(_SEQ_NAMES):]
    i = pl.program_id(0) % nt
    _seq_start(i, scratch, tt)
    _seq_tile(x_ref, g_ref, w_ref, bff_ref, hc_ref, cw_ref, cp_ref, hn_ref, refs, scratch, tt, first_of=depth)
    _seq_finish(i, nt, refs, scratch)


def _seq_start(i, scratch, tt):
    ext_ref, st_ref = scratch[:2]

    @pl.when(i == 0)
    def _():
        _conv_start(ext_ref, jnp.zeros((HALO, G), F32), tt)
        st_ref[...] = jnp.zeros(st_ref.shape, F32)


def _seq_finish(i, nt, refs, scratch):
    @pl.when(i == nt - 1)
    def _():
        refs['st_out'][0] = scratch[1][...]


def _seq_tile(x_ref, g_ref, w_ref, bff_ref, hc_ref, cw_ref, cp_ref, hn_ref, refs, scratch, tt,
              between=lambda: None, first_of=None):
    ext_ref, st_ref, kpad, vpad, bpad = scratch
    vals = {}
    for name, val in _proj_groups(x_ref, g_ref, w_ref, bff_ref, hc_ref):
        if name in _PROJ_KV:
            vt = val.T
            refs[name + '_t'][0, 0] = vt
            for d in range(1, first_of or 1):
                refs[name + '_t'][d, 0] = jnp.zeros_like(vt)
            refs[name + '_tb'][0, 0] = vt.astype(BF16)
        elif name == 'flf':
            refs[name][0] = val.T[0:8, :]
        elif name in _SEQ_HGRN_IN:
            vals[name] = val
        else:
            refs[name][...] = val.astype(refs[name].dtype)
            if name == 'u':
                vals[name] = val
    between()
    _conv_tile(vals['u'], cw_ref, cp_ref, refs['conv_o'], ext_ref, tt)
    between()
    _hgrn_tile(vals['hq'], vals['hk'], vals['hv'], vals['hlf'], vals['hgate'], hn_ref,
               refs['hg_o'], st_ref, kpad, vpad, bpad, tt, between)


def _seq_outputs(t, depth, b, l, tt, tile_w, layer, tile_of):
    nt = l // tt
    per = tile_w // tt
    f32o = jax.ShapeDtypeStruct((t, G), F32)
    b16o = jax.ShapeDtypeStruct((t, G), BF16)
    stacked = jax.ShapeDtypeStruct((depth, b, G, l), F32)
    tiled = jax.ShapeDtypeStruct((b, l // tile_w, G, tile_w), BF16)
    n_rows = len(_SEQ_ROW_F32) + len(_SEQ_ROW_BF16)
    outs = ([f32o] * len(_SEQ_ROW_F32) + [b16o] * len(_SEQ_ROW_BF16)
            + [jax.ShapeDtypeStruct((b, 8, l), F32)] + [stacked, tiled] * len(_PROJ_KV)
            + [jax.ShapeDtypeStruct((b, G, G), F32)])
    seq = lambda i: tile_of(i) // nt
    pos = lambda i: tile_of(i) % nt
    stacked_spec = (pl.BlockSpec((depth, 1, G, tt), lambda i: (0, seq(i), 0, pos(i))) if layer is None else
                    pl.BlockSpec((1, 1, G, tt), lambda i: (layer, seq(i), 0, pos(i))))
    ospecs = ([pl.BlockSpec((tt, G), lambda i: (tile_of(i), 0))] * n_rows
              + [pl.BlockSpec((1, 8, tt), lambda i: (seq(i), 0, pos(i)))]
              + [stacked_spec,
                 pl.BlockSpec((1, 1, G, tt), lambda i: (seq(i), pos(i) // per, 0, pos(i) % per))] * len(_PROJ_KV)
              + [pl.BlockSpec((1, G, G), lambda i: (seq(i), 0, 0))])
    return outs, ospecs, n_rows + 1


def _seq_scratch(tt):
    return ([pltpu.VMEM((tt + 40, G), F32), pltpu.VMEM((G, G), F32)]
            + [pltpu.VMEM((tt + SUB, G), F32)] * 3)


def _seq_in_specs(w, resident):
    fix = lambda i: (0, 0)
    spec = (lambda shape: pl.BlockSpec(shape, fix, pipeline_mode=pl.Buffered(1))) if resident else \
           (lambda shape: pl.BlockSpec(shape, fix))
    return [pl.BlockSpec((1, D_MODEL), fix), spec((D_MODEL, w.shape[1])), pl.BlockSpec((1, FF_PAD), fix),
            pl.BlockSpec((4, G), fix), pl.BlockSpec((CONV_W, G), fix), pl.BlockSpec((3, G), fix),
            pl.BlockSpec((1, G), fix)]


def _proj_seq(x, g, w, bff, hc, cw, cp, hn, *, depth, b, l, tt):
    t = x.shape[0]
    outs, ospecs, _ = _seq_outputs(t, depth, b, l, tt, tt, None, lambda i: i)
    res = pl.pallas_call(
        functools.partial(_proj_seq_kernel, nt=l // tt, tt=tt, depth=depth),
        grid=(t // tt,),
        in_specs=[pl.BlockSpec((tt, D_MODEL), lambda i: (i, 0))] + _seq_in_specs(w, False),
        out_specs=ospecs,
        out_shape=outs,
        scratch_shapes=_seq_scratch(tt),
        compiler_params=_params("arbitrary"),
        name="proj_seq",
    )(x, g, w, bff, hc, cw, cp, hn)
    return dict(zip(_SEQ_NAMES, res))


def _mlp_proj_kernel(x1_ref, o_ref, wmo_ref, gf_ref, wup_ref, wdn_ref,
                     g_ref, w_ref, bff_ref, hc_ref, cw_ref, cp_ref, hn_ref, *rest, n_alias, nt, tt, tf):
    rest = rest[n_alias:]
    x3_ref = rest[0]
    refs = dict(zip(_SEQ_NAMES, rest[1:1 + len(_SEQ_NAMES)]))
    xprev_ref = rest[1 + len(_SEQ_NAMES)]
    scratch = rest[2 + len(_SEQ_NAMES):]
    step = pl.program_id(0)
    i = jnp.maximum(step - 1, 0) % nt

    @pl.when(step == 0)
    def _():
        xprev_ref[...] = jnp.zeros(xprev_ref.shape, F32)

    _seq_start(i, scratch, tt)
    mlp = {}
    pieces = _mlp_pieces(x1_ref, o_ref, wmo_ref, gf_ref, wup_ref, wdn_ref, tf, mlp)

    def between():
        if pieces:
            pieces.pop(0)()

    _seq_tile(xprev_ref, g_ref, w_ref, bff_ref, hc_ref, cw_ref, cp_ref, hn_ref, refs, scratch, tt, between)
    while pieces:
        between()
    x3_ref[...] = mlp['x3']
    xprev_ref[...] = mlp['x3']
    _seq_finish(i, nt, refs, scratch)


def _mlp_proj(x1, o, wmo, gf, wup, wdn, g, w, bff, hc, cw, cp, hn, bufs, *, layer, depth, b, l, tt, tile_w, tf):
    t = x1.shape[0]
    n_tiles = t // tt
    mlp_tile = lambda s: (jnp.minimum(s, n_tiles - 1), 0)
    fix = lambda s: (0, 0)
    resident = lambda shape: pl.BlockSpec(shape, fix, pipeline_mode=pl.Buffered(1))
    outs, ospecs, first_stacked = _seq_outputs(t, depth, b, l, tt, tile_w, layer, lambda s: jnp.maximum(s - 1, 0))
    n_in = 13
    alias_args = list(bufs)
    aliases = {n_in + k: 1 + first_stacked + 2 * k for k in range(len(_PROJ_KV))}
    res = pl.pallas_call(
        functools.partial(_mlp_proj_kernel, n_alias=len(alias_args), nt=l // tt, tt=tt, tf=tf),
        grid=(n_tiles + 1,),
        in_specs=[pl.BlockSpec((tt, D_MODEL), mlp_tile), pl.BlockSpec((tt, D_MODEL), mlp_tile),
                  resident((D_MODEL, D_MODEL)), pl.BlockSpec((1, D_MODEL), fix),
                  resident((D_MODEL, D_FF)), resident((D_FF, D_MODEL))] + _seq_in_specs(w, True)
                 + [pl.BlockSpec(memory_space=pl.ANY)] * len(alias_args),
        out_specs=[pl.BlockSpec((tt, D_MODEL), mlp_tile)] + ospecs,
        out_shape=[jax.ShapeDtypeStruct((t, D_MODEL), F32)] + outs,
        input_output_aliases=aliases,
        scratch_shapes=[pltpu.VMEM((tt, D_MODEL), F32)] + _seq_scratch(tt),
        compiler_params=_params("arbitrary"),
        name="mlp_proj",
    )(x1, o, wmo, gf, wup, wdn, g, w, bff, hc, cw, cp, hn, *alias_args)
    return res[0], dict(zip(_SEQ_NAMES, res[1:]))


def _cumsum_kernel(x_ref, o_ref, *, scale):
    x = x_ref[...]
    n = x.shape[1]
    lane = lax.broadcasted_iota(jnp.int32, x.shape, 1)
    sh = 1
    while sh < n:
        x = x + jnp.where(lane >= sh, pltpu.roll(x, sh, axis=1), 0.0)
        sh *= 2
    o_ref[...] = x * scale


def _cumsum_lanes(x, scale):
    return pl.pallas_call(
        functools.partial(_cumsum_kernel, scale=scale),
        out_shape=jax.ShapeDtypeStruct(x.shape, F32),
        compiler_params=pltpu.CompilerParams(vmem_limit_bytes=VMEM_LIMIT_BYTES),
        name="cumsum",
    )(x)


CONV_BASE = 32 - HALO


def _conv_start(ext_ref, hist, tt):
    ext_ref[0:CONV_BASE, :] = jnp.zeros((CONV_BASE, G), F32)
    ext_ref[CONV_BASE:32, :] = hist
    ext_ref[32 + tt:40 + tt, :] = jnp.zeros((8, G), F32)


def _conv_tile(u, w_ref, p_ref, o_ref, ext_ref, tt):
    base = CONV_BASE
    ext_ref[32:32 + tt, :] = u
    acc = None
    for r in range(8):
        part = None
        for m in range((base + CONV_W - 1) // 8 + 1):
            j = 8 * m + r - base
            if 0 <= j < CONV_W:
                term = w_ref[j:j + 1, :] * ext_ref[8 * m:8 * m + tt + 8, :]
                part = term if part is None else part + term
        part = part[r:r + tt, :]
        acc = part if acc is None else acc + part
    yf = acc + p_ref[0:1, :]
    mu = jnp.mean(yf, axis=-1, keepdims=True)
    d = yf - mu
    var = jnp.mean(d * d, axis=-1, keepdims=True)
    yn = d * lax.rsqrt(var + EPS) * p_ref[1:2, :] + p_ref[2:3, :]
    o_ref[...] = (yn * _sigmoid(yn)).astype(BF16)
    ext_ref[base:32, :] = ext_ref[tt + base:tt + 32, :]


def _conv_kernel(u_ref, hist_ref, w_ref, p_ref, o_ref, ext_ref, *, tt):
    @pl.when(pl.program_id(1) == 0)
    def _():
        _conv_start(ext_ref, hist_ref[0], tt)

    _conv_tile(u_ref[...], w_ref, p_ref, o_ref, ext_ref, tt)


def _conv(u, hist, w, p, *, row_off, b, l, tt):
    nt = l // tt
    off = row_off // tt
    return pl.pallas_call(
        functools.partial(_conv_kernel, tt=tt),
        grid=(b, nt),
        in_specs=[pl.BlockSpec((tt, G), lambda bi, i: (off + bi * nt + i, 0)),
                  pl.BlockSpec((1, HALO, G), lambda bi, i: (bi, 0, 0)),
                  pl.BlockSpec((CONV_W, G), lambda bi, i: (0, 0)),
                  pl.BlockSpec((3, G), lambda bi, i: (0, 0))],
        out_specs=pl.BlockSpec((tt, G), lambda bi, i: (bi * nt + i, 0)),
        out_shape=jax.ShapeDtypeStruct((b * l, G), BF16),
        scratch_shapes=[pltpu.VMEM((tt + 40, G), F32)],
        compiler_params=_params("arbitrary", "arbitrary"),
        name="conv",
    )(u, hist, w, p)


def _lanes(x, w):
    if w < LANES:
        return x[:, :w]
    return x if w == LANES else jnp.concatenate([x] * (w // LANES), axis=1)


def _scores(q4, k, t_layout):
    return _dot(q4, k) if t_layout else _dot_nt(q4, k)


def _weighted(p4, v, t_layout):
    return _dot_nt(p4, v) if t_layout else _dot(p4, v)


def _fox_kernel(*refs, tq, tkp, n_past, cur_t):
    if n_past:
        (q_ref, kc_ref, vc_ref, ckc_ref, cq_ref, kp_ref, vp_ref, ckp_ref,
         o_ref, m_ref, l_ref, acc_ref) = refs
    else:
        q_ref, kc_ref, vc_ref, ckc_ref, cq_ref, o_ref, m_ref, l_ref, acc_ref = refs
    i = pl.program_id(1)
    lane_head = lax.broadcasted_iota(jnp.int32, (tq, G), 1) >> HD_SHIFT
    q4 = _stack_heads(q_ref[...], lane_head)
    cq = cq_ref[...]
    cqb = [jnp.broadcast_to(cq[:, h:h + 1], (tq, LANES)) for h in range(H)]
    tri = (lax.broadcasted_iota(jnp.int32, (tq, tq), 1)
           <= lax.broadcasted_iota(jnp.int32, (tq, tq), 0))
    m_ref[...] = jnp.full(m_ref.shape, NEG_BIG, F32)
    l_ref[...] = jnp.zeros(l_ref.shape, F32)
    acc_ref[...] = jnp.zeros(acc_ref.shape, F32)

    def step(k, v, ck, masked, t_layout):
        s = _scores(q4, k, t_layout)
        w = s.shape[1]
        ps = []
        for h in range(H):
            sl = slice(h * tq, (h + 1) * tq)
            sh = s[sl] - ck[h:h + 1, :]
            if masked:
                sh = jnp.where(tri, sh, NEG_BIG)
            m_old = m_ref[sl]
            m_new = jnp.maximum(m_old, jnp.max(sh, axis=-1, keepdims=True) + cqb[h])
            p = jnp.exp2(sh - _lanes(m_new - cqb[h], w))
            alpha = jnp.exp2(m_old - m_new)
            l_ref[sl] = alpha * l_ref[sl] + jnp.sum(p, axis=-1, keepdims=True)
            m_ref[sl] = m_new
            acc_ref[sl] = acc_ref[sl] * _lanes(alpha, G)
            ps.append(p.astype(BF16))
        acc_ref[...] += _weighted(jnp.concatenate(ps, axis=0), v, t_layout)

    for j in range(n_past):
        sl = slice(j * tkp, (j + 1) * tkp)
        step(kp_ref[0, 0, :, sl].astype(BF16), vp_ref[0, 0, :, sl].astype(BF16), ckp_ref[0, j], False, True)

    if cur_t:
        def cur_body(j, c):
            step(kc_ref[0, j], vc_ref[0, j], ckc_ref[0, j], False, True)
            return c
        lax.fori_loop(0, i, cur_body, 0)
        step(kc_ref[0, i], vc_ref[0, i], ckc_ref[0, i], True, True)
    else:
        step(kc_ref[...], vc_ref[...], ckc_ref[0, 0], True, False)

    acc4 = jnp.concatenate(
        [acc_ref[h * tq:(h + 1) * tq] * _lanes(1.0 / l_ref[h * tq:(h + 1) * tq], G) for h in range(H)],
        axis=0)
    o_ref[...] = _unstack_heads(acc4, lane_head, tq).astype(BF16)


def _sb_kernel(*refs, tq, tkp, n_past, cur_t):
    if n_past:
        q_ref, kc_ref, vc_ref, kp_ref, vp_ref, o_ref, r_ref, acc_ref = refs
    else:
        q_ref, kc_ref, vc_ref, o_ref, r_ref, acc_ref = refs
    i = pl.program_id(1)
    lane_head = lax.broadcasted_iota(jnp.int32, (tq, G), 1) >> HD_SHIFT
    q4 = _stack_heads(q_ref[...], lane_head)
    r_ref[...] = jnp.zeros(r_ref.shape, F32)
    acc_ref[...] = jnp.zeros(acc_ref.shape, F32)

    def later_matrix(n):
        later = (lax.broadcasted_iota(jnp.int32, (n, n), 0) > lax.broadcasted_iota(jnp.int32, (n, n), 1))
        return jnp.where(later, 1.0, 0.0).astype(BF16)

    widths = {min(tq, SB_BLOCK)} | ({min(tkp, SB_BLOCK)} if n_past else set())
    later = {n: later_matrix(n) for n in widths}

    tri = (lax.broadcasted_iota(jnp.int32, (tq, tq), 1)
           < lax.broadcasted_iota(jnp.int32, (tq, tq), 0))
    valid = jnp.concatenate([tri] * H, axis=0)
    groups = ATTN_GROUPS if H * tq // ATTN_GROUPS >= ATTN_GROUP_MIN_ROWS else 1

    def step(k, v, masked, t_layout):
        for g in range(groups):
            rows = slice(g * (H * tq // groups), (g + 1) * (H * tq // groups))
            z = _scores(q4[rows], k, t_layout)
            w = z.shape[1]
            sp = jnp.where(z > SOFTPLUS_LINEAR, z, jnp.log2(1.0 + jnp.exp2(z)))
            u = jnp.where(valid[rows], sp, 0.0) if masked else sp
            bw = min(w, SB_BLOCK)
            r_run = r_ref[rows]
            rests = [None] * (w // bw)
            for blk in reversed(range(w // bw)):
                ub = u[:, blk * bw:(blk + 1) * bw]
                rests[blk] = _dot(ub.astype(BF16), later[bw]) + _lanes(r_run, bw)
                r_run = r_run + jnp.sum(ub, axis=-1, keepdims=True)
            rest = rests[0] if len(rests) == 1 else jnp.concatenate(rests, axis=1)
            a = jnp.exp2(z - sp - rest)
            if masked:
                a = jnp.where(valid[rows], a, 0.0)
            acc_ref[rows] += _weighted(a.astype(BF16), v, t_layout)
            r_ref[rows] = r_run

    if cur_t:
        step(kc_ref[0, i], vc_ref[0, i], True, True)

        def cur_body(n, c):
            j = i - 1 - n
            step(kc_ref[0, j], vc_ref[0, j], False, True)
            return c
        lax.fori_loop(0, i, cur_body, 0)
    else:
        step(kc_ref[...], vc_ref[...], True, False)

    for j in reversed(range(n_past)):
        sl = slice(j * tkp, (j + 1) * tkp)
        step(kp_ref[0, 0, :, sl].astype(BF16), vp_ref[0, 0, :, sl].astype(BF16), False, True)

    o_ref[...] = _unstack_heads(acc_ref[...], lane_head, tq).astype(BF16)


def _attention(kind, q, kc, vc, extra, past, *, layer, b, l, tq, tkp):
    nq = l // tq
    cur_t = kc.ndim == 4
    assert cur_t or nq == 1
    if cur_t:
        kv_spec = pl.BlockSpec((1, nq, G, tq), lambda bi, i: (bi, 0, 0, 0))
    else:
        kv_spec = pl.BlockSpec((l, G), lambda bi, i: (bi, 0))
    in_specs = [pl.BlockSpec((tq, G), lambda bi, i: (bi * nq + i, 0)), kv_spec, kv_spec]
    args = [q, kc, vc]
    if kind == 'fox':
        ckc, cq = extra
        in_specs += [pl.BlockSpec((1, nq, H, tq), lambda bi, i: (bi, 0, 0, 0)),
                     pl.BlockSpec((tq, H), lambda bi, i: (bi * nq + i, 0))]
        args += [ckc, cq]
    n_past = 0
    if past is not None:
        plen = past[0].shape[3]
        n_past = plen // tkp
        cache_spec = pl.BlockSpec((1, 1, G, plen), lambda bi, i: (layer, bi, 0, 0))
        in_specs += [cache_spec, cache_spec]
        args += [past[0], past[1]]
        if kind == 'fox':
            in_specs.append(pl.BlockSpec((1, n_past, H, tkp), lambda bi, i: (bi, 0, 0, 0)))
            args.append(past[2])
    stat = pltpu.VMEM((H * tq, LANES), F32)
    acc = pltpu.VMEM((H * tq, G), F32)
    body = _fox_kernel if kind == 'fox' else _sb_kernel
    return pl.pallas_call(
        functools.partial(body, tq=tq, tkp=tkp, n_past=n_past, cur_t=cur_t),
        grid=(b, nq),
        in_specs=in_specs,
        out_specs=pl.BlockSpec((tq, G), lambda bi, i: (bi * nq + i, 0)),
        out_shape=jax.ShapeDtypeStruct((b * l, G), BF16),
        scratch_shapes=[stat, stat, acc] if kind == 'fox' else [stat, acc],
        compiler_params=_params("parallel", "arbitrary"),
        name=kind,
    )(*args)


def _hgrn_tile(q, kk, v, lf, gate, hn_ref, o_ref, st_ref, kpad, vpad, bpad, tt, between=lambda: None):
    nc = tt // CHUNK
    row = lax.broadcasted_iota(jnp.int32, (tt, G), 0)
    r64 = row & (CHUNK - 1)
    rsub = row & (SUB - 1)
    lane_head = lax.broadcasted_iota(jnp.int32, (tt, G), 1) >> HD_SHIFT
    same_head = (lax.broadcasted_iota(jnp.int32, (G, G), 0) >> HD_SHIFT
                 == lax.broadcasted_iota(jnp.int32, (G, G), 1) >> HD_SHIFT)
    bd = jnp.where(same_head, 1.0, 0.0).astype(BF16)

    b = lf * LOG2E
    sh = 1
    while sh < CHUNK:
        b = b + jnp.where(r64 >= sh, pltpu.roll(b, sh, axis=0), 0.0)
        sh *= 2

    def chunk_row(r):
        return jnp.concatenate(
            [jnp.broadcast_to(b[c * CHUNK + r:c * CHUNK + r + 1, :], (CHUNK, G)) for c in range(nc)], axis=0)

    zpad = jnp.zeros((SUB, G), F32)
    kpad[0:SUB, :] = zpad
    vpad[0:SUB, :] = zpad
    bpad[0:SUB, :] = zpad
    kpad[SUB:SUB + tt, :] = kk
    vpad[SUB:SUB + tt, :] = v
    bpad[SUB:SUB + tt, :] = b
    o = jnp.zeros((tt, G), F32)
    for d in range(SUB):
        ks = kpad[SUB - d:SUB - d + tt, :]
        vs = vpad[SUB - d:SUB - d + tt, :]
        bs = bpad[SUB - d:SUB - d + tt, :]
        p = q * ks * jnp.exp2(jnp.minimum(b - bs, 0.0))
        p = jnp.where(rsub >= d, p, 0.0)
        o = o + _dot(p.astype(BF16), bd) * vs
        if d == SUB // 2 - 1:
            between()
    between()

    nsub = CHUNK // SUB
    refs_b = [chunk_row(SUB * n - 1) for n in range(1, nsub)]
    sub = r64 >> SUB_SHIFT
    rq = b
    for n, rb in enumerate(refs_b, start=1):
        rq = jnp.where(sub == n, rb, rq)
    qt = q * jnp.exp2(jnp.minimum(b - rq, 0.0))
    gr = SUB * nc
    lane_head_g = lax.broadcasted_iota(jnp.int32, (gr, G), 1) >> HD_SHIFT
    g_row = lax.broadcasted_iota(jnp.int32, (gr, tt), 0)
    g_col = lax.broadcasted_iota(jnp.int32, (gr, tt), 1)
    key_sub = jnp.where(g_row >> SUB_SHIFT == g_col >> CHUNK_SHIFT, (g_col & (CHUNK - 1)) >> SUB_SHIFT, nsub)
    key_sub4 = jnp.concatenate([key_sub] * H, axis=0)
    a_all = []
    for n, rb in enumerate(refs_b, start=1):
        qn = jnp.concatenate([qt[c * CHUNK + SUB * n:c * CHUNK + SUB * (n + 1)] for c in range(nc)], axis=0)
        q4n = _stack_heads(qn, lane_head_g).astype(BF16)
        kt = (kk * jnp.exp2(jnp.minimum(rb - b, 0.0))).astype(BF16)
        a_all.append(jnp.where(key_sub4 < n, _dot_nt(q4n, kt), 0.0).astype(BF16))
    vb = v.astype(BF16)
    o_all = _dot(jnp.concatenate(a_all, axis=0), vb)
    o_sub = [_unstack_heads(o_all[(n - 1) * H * gr:n * H * gr], lane_head_g, gr) for n in range(1, nsub)]
    pieces = []
    for c in range(nc):
        pieces.append(jnp.zeros((SUB, G), F32))
        pieces += [o_sub[n - 1][c * SUB:(c + 1) * SUB] for n in range(1, nsub)]
    o = o + jnp.concatenate(pieces, axis=0)

    between()

    blast = chunk_row(CHUNK - 1)
    qs = (q * jnp.exp2(b)).astype(BF16)
    kd = (kk * jnp.exp2(jnp.minimum(blast - b, 0.0))).astype(BF16)
    dec = jnp.exp2(blast)
    outs = []
    for c in range(nc):
        sl = slice(c * CHUNK, (c + 1) * CHUNK)
        st = st_ref[...]
        outs.append(_dot_nt(qs[sl], st.astype(BF16)))
        upd = _dot_tn(vb[sl], kd[sl])
        st_ref[...] = st * dec[c * CHUNK:c * CHUNK + 1, :] + jnp.where(same_head, upd, 0.0)
    o = o + jnp.concatenate(outs, axis=0)

    sq = o * o
    hi = sq.astype(BF16)
    lo = (sq - hi.astype(F32)).astype(BF16)
    ms = (_dot(hi, bd) + _dot(lo, bd)) * (1.0 / HD)
    o_ref[...] = (o * lax.rsqrt(ms + EPS) * hn_ref[...] * (gate * _sigmoid(gate))).astype(BF16)


def _hgrn_kernel(*refs, tt, has_init):
    if has_init:
        (q_ref, k_ref, v_ref, lf_ref, gate_ref, hn_ref, st0_ref,
         o_ref, sto_ref, st_ref, kpad, vpad, bpad) = refs
    else:
        (q_ref, k_ref, v_ref, lf_ref, gate_ref, hn_ref,
         o_ref, sto_ref, st_ref, kpad, vpad, bpad) = refs
    i = pl.program_id(1)

    @pl.when(i == 0)
    def _():
        if has_init:
            st_ref[...] = st0_ref[0]
        else:
            st_ref[...] = jnp.zeros(st_ref.shape, F32)

    _hgrn_tile(q_ref[...], k_ref[...], v_ref[...], lf_ref[...], gate_ref[...], hn_ref,
               o_ref, st_ref, kpad, vpad, bpad, tt)

    @pl.when(i == pl.num_programs(1) - 1)
    def _():
        sto_ref[0] = st_ref[...]


def _hgrn(q, k, v, lf, gate, hn, st0, *, row_off, b, l, tt):
    nt = l // tt
    off = row_off // tt
    tok = pl.BlockSpec((tt, G), lambda bi, i: (off + bi * nt + i, 0))
    in_specs = [tok, tok, tok, tok, tok, pl.BlockSpec((1, G), lambda bi, i: (0, 0))]
    args = [q, k, v, lf, gate, hn]
    if st0 is not None:
        in_specs.append(pl.BlockSpec((1, G, G), lambda bi, i: (bi, 0, 0)))
        args.append(st0)
    return pl.pallas_call(
        functools.partial(_hgrn_kernel, tt=tt, has_init=st0 is not None),
        grid=(b, nt),
        in_specs=in_specs,
        out_specs=[pl.BlockSpec((tt, G), lambda bi, i: (bi * nt + i, 0)),
                   pl.BlockSpec((1, G, G), lambda bi, i: (bi, 0, 0))],
        out_shape=[jax.ShapeDtypeStruct((b * l, G), BF16), jax.ShapeDtypeStruct((b, G, G), F32)],
        scratch_shapes=[pltpu.VMEM((G, G), F32)] + [pltpu.VMEM((tt + SUB, G), F32)] * 3,
        compiler_params=_params("arbitrary", "arbitrary"),
        name="hgrn",
    )(*args)


def _c1_kernel(x_ref, m0_ref, m1_ref, m2_ref, m3_ref, wo_ref, g_ref, wq_ref, x1_ref, q_ref):
    acc = x_ref[...]
    for p, m_ref in enumerate((m0_ref, m1_ref, m2_ref, m3_ref)):
        acc = acc + _dot(m_ref[...], wo_ref[p * G:(p + 1) * G, :])
    x1_ref[...] = acc
    h = _rms(acc, g_ref[...]).astype(BF16)
    q_ref[...] = (_dot(h, wq_ref[...]) * (MEM_HD ** -0.5)).astype(BF16)


def _c1(x, mix, wo, g, wq, tt):
    t = x.shape[0]
    row = lambda i: (i, 0)
    fix = lambda i: (0, 0)
    return pl.pallas_call(
        _c1_kernel,
        grid=(t // tt,),
        in_specs=[pl.BlockSpec((tt, D_MODEL), row)] + [pl.BlockSpec((tt, G), row)] * 4
                 + [pl.BlockSpec((D_MODEL, D_MODEL), fix), pl.BlockSpec((1, D_MODEL), fix),
                    pl.BlockSpec((D_MODEL, D_MODEL), fix)],
        out_specs=[pl.BlockSpec((tt, D_MODEL), row), pl.BlockSpec((tt, D_MODEL), row)],
        out_shape=[jax.ShapeDtypeStruct((t, D_MODEL), F32), jax.ShapeDtypeStruct((t, D_MODEL), BF16)],
        compiler_params=_params("parallel"),
        name="outproj_memq",
    )(x, *mix, wo, g, wq)


def _memkv_kernel(m_ref, wk_ref, wv_ref, k_ref, kb_ref, v_ref, vb_ref):
    m = m_ref[...].astype(BF16)
    k = _dot(m, wk_ref[...])
    k_ref[...] = k
    kb_ref[...] = k.astype(BF16)
    v = _dot(m, wv_ref[...])
    v_ref[...] = v
    vb_ref[...] = v.astype(BF16)


def _memkv(mem, wk, wv, tt):
    t = mem.shape[0]
    row = lambda i: (i, 0)
    fix = lambda i: (0, 0)
    f32o = jax.ShapeDtypeStruct((t, D_MODEL), F32)
    b16o = jax.ShapeDtypeStruct((t, D_MODEL), BF16)
    return pl.pallas_call(
        _memkv_kernel,
        grid=(t // tt,),
        in_specs=[pl.BlockSpec((tt, D_MODEL), row), pl.BlockSpec((D_MODEL, D_MODEL), fix),
                  pl.BlockSpec((D_MODEL, D_MODEL), fix)],
        out_specs=[pl.BlockSpec((tt, D_MODEL), row)] * 4,
        out_shape=[f32o, b16o, f32o, b16o],
        compiler_params=_params("parallel"),
        name="memkv",
    )(mem, wk, wv)


def _memattn_kernel(q_ref, k_ref, v_ref, o_ref):
    outs = []
    for h in range(MEM_HEADS):
        sl = slice(h * MEM_HD, (h + 1) * MEM_HD)
        s = _dot_nt(q_ref[:, sl], k_ref[:, sl].astype(BF16))
        p = jnp.exp(s - jnp.max(s, axis=-1, keepdims=True))
        den = jnp.sum(p, axis=-1, keepdims=True)
        outs.append(_dot(p.astype(BF16), v_ref[:, sl].astype(BF16)) * (1.0 / den))
    o_ref[...] = jnp.concatenate(outs, axis=-1).astype(BF16)


def _memattn(q, k, v, *, row_off, b, l, tq):
    nq = l // tq
    off = row_off // tq
    return pl.pallas_call(
        _memattn_kernel,
        grid=(b, nq),
        in_specs=[pl.BlockSpec((tq, D_MODEL), lambda bi, i: (off + bi * nq + i, 0)),
                  pl.BlockSpec((N_MEM, D_MODEL), lambda bi, i: (bi, 0)),
                  pl.BlockSpec((N_MEM, D_MODEL), lambda bi, i: (bi, 0))],
        out_specs=pl.BlockSpec((tq, D_MODEL), lambda bi, i: (bi * nq + i, 0)),
        out_shape=jax.ShapeDtypeStruct((b * l, D_MODEL), BF16),
        compiler_params=_params("parallel", "parallel"),
        name="memattn",
    )(q, k, v)


def _mlp_pieces(x1_ref, o_ref, wmo_ref, g_ref, wup_ref, wdn_ref, tf, out):
    def first():
        x2 = x1_ref[...] + _dot(o_ref[...], wmo_ref[...])
        out['h'] = _rms(x2, g_ref[...]).astype(BF16)
        out['x3'] = x2

    def chunk(f):
        a = jnp.maximum(_dot(out['h'], wup_ref[:, f * tf:(f + 1) * tf]), 0.0)
        out['x3'] = out['x3'] + _dot((a * a).astype(BF16), wdn_ref[f * tf:(f + 1) * tf, :])

    return [first] + [functools.partial(chunk, f) for f in range(D_FF // tf)]


def _mlp_tile(x1_ref, o_ref, wmo_ref, g_ref, wup_ref, wdn_ref, tf):
    out = {}
    for piece in _mlp_pieces(x1_ref, o_ref, wmo_ref, g_ref, wup_ref, wdn_ref, tf, out):
        piece()
    return out['x3']


def _c3_kernel(x1_ref, o_ref, wmo_ref, g_ref, wup_ref, wdn_ref, gf_ref, y_ref, *, final, tf):
    x3 = _mlp_tile(x1_ref, o_ref, wmo_ref, g_ref, wup_ref, wdn_ref, tf)
    y_ref[...] = _rms(x3, gf_ref[...]) if final else x3


def _c3(x1, o, wmo, g, wup, wdn, gf, *, tt, tf, final):
    t = x1.shape[0]
    row = lambda i: (i, 0)
    fix = lambda i: (0, 0)
    resident = lambda shape: pl.BlockSpec(shape, fix, pipeline_mode=pl.Buffered(1))
    return pl.pallas_call(
        functools.partial(_c3_kernel, final=final, tf=tf),
        grid=(t // tt,),
        in_specs=[pl.BlockSpec((tt, D_MODEL), row), pl.BlockSpec((tt, D_MODEL), row),
                  resident((D_MODEL, D_MODEL)), pl.BlockSpec((1, D_MODEL), fix),
                  resident((D_MODEL, D_FF)), resident((D_FF, D_MODEL)),
                  pl.BlockSpec((1, D_MODEL), fix)],
        out_specs=pl.BlockSpec((tt, D_MODEL), row),
        out_shape=jax.ShapeDtypeStruct((t, D_MODEL), F32),
        compiler_params=_params("parallel"),
        name="memout_mlp",
    )(x1, o, wmo, g, wup, wdn, gf)


def _block_diag_t(s):
    b = s.shape[0]
    eye = jnp.eye(H, dtype=s.dtype)
    st = jnp.swapaxes(s, 2, 3)
    return jnp.einsum('bhvk,hg->bhvgk', st, eye).reshape(b, G, G)


def _unblock_diag_t(st):
    b = st.shape[0]
    s5 = st.reshape(b, H, HD, H, HD)
    d = jnp.stack([s5[:, h, :, h, :] for h in range(H)], axis=1)
    return jnp.swapaxes(d, 2, 3)


def _row_cumsum(lf, tile, scale):
    b, s, _ = lf.shape
    sp = -(-s // tile) * tile
    x = jnp.swapaxes(lf, 1, 2).reshape(b * H, s)
    x = jnp.pad(x, ((0, 0), (0, sp - s)))
    return _cumsum_lanes(x, scale).reshape(b, H, sp)[:, :, :s]


def _tiles(c, n, t):
    b = c.shape[0]
    return jnp.swapaxes(c.reshape(b, H, n, t), 1, 2)


def kernel(x_prompt, x_sample, mem_prompt, cache_conv, cache_fox_k, cache_fox_v, cache_fox_logf,
           state_hgrn, cache_sb_k, cache_sb_v, cache_mem_k, cache_mem_v, norm_mix, w_in, b_fox_f,
           conv_w, conv_b, conv_ln_g, conv_ln_b, hgrn_lb, hgrn_norm, w_out, norm_mem, w_mq, w_mk,
           w_mv, w_mo, norm_ffn, w_up, w_down, norm_final):
    bp, sp, _ = x_prompt.shape
    bs, ls, _ = x_sample.shape
    depth = w_in.shape[0]
    past = cache_fox_k.shape[2]
    tp = bp * sp
    ts = bs * ls
    tok = 512
    tq_p = 512
    tk_past = 512
    tt_fused = 256

    sm = jax.nn.softmax(hgrn_lb.astype(F32), axis=0)
    lower = jnp.clip(jnp.cumsum(sm, axis=0) - sm[0], 0.0, 1.0 - 1e-6)
    pos = lower > 0.0
    hconst = jnp.stack([jnp.log1p(-lower), jnp.log(jnp.where(pos, lower, 1.0)),
                        pos.astype(F32), 1.0 - lower], axis=1)

    o_ff = 5 * G
    w_in_r = jnp.concatenate(
        [w_in[:, :, :o_ff], w_in[:, :, o_ff + H:],
         jnp.pad(w_in[:, :, o_ff:o_ff + H], ((0, 0), (0, 0), (0, FF_PAD - H)))], axis=-1).astype(BF16)
    bff = jnp.pad(b_fox_f, ((0, 0), (0, FF_PAD - H)))[:, None, :]
    w_out_b = w_out.astype(BF16)
    w_mq_b = w_mq.astype(BF16)
    w_mk_b = w_mk.astype(BF16)
    w_mv_b = w_mv.astype(BF16)
    w_mo_b = w_mo.astype(BF16)
    w_up_b = w_up.astype(BF16)
    w_down_b = w_down.astype(BF16)
    hn = jnp.tile(hgrn_norm, (1, H))[:, None, :]
    conv_p3 = jnp.stack([conv_b, conv_ln_g, conv_ln_b], axis=1)

    def seq_last(c):
        return jnp.transpose(c, (0, 1, 3, 4, 2)).reshape(depth, bs, G, past)

    fkc, fvc, skc, svc = (seq_last(c) for c in (cache_fox_k, cache_fox_v, cache_sb_k, cache_sb_v))

    st0_all = _block_diag_t(state_hgrn.astype(F32).reshape(depth * bs, H, HD, HD)).reshape(depth, bs, G, G)
    xp = x_prompt.reshape(tp, D_MODEL)
    xs = x_sample.reshape(ts, D_MODEL)
    mem2 = mem_prompt.reshape(bp * N_MEM, D_MODEL)

    outs = {n: [] for n in ('conv_p', 'flf_p', 'hg_p', 'mk_p', 'mv_p',
                            'conv_s', 'fk_s', 'fv_s', 'flf_s', 'hg_s', 'sk_s', 'sv_s')}
    def seq_params(l):
        return (norm_mix[l][None], w_in_r[l], bff[l], hconst[l], conv_w[l], conv_p3[l], hn[l])

    pp = _proj_seq(xp, *seq_params(0), depth=depth, b=bp, l=sp, tt=tq_p)
    for l in range(depth):
        bufs = tuple(pp[n + '_t'] for n in _PROJ_KV)
        conv_op, hg_op, st_p = pp['conv_o'], pp['hg_o'], pp['st_out']
        u_p = pp['u']
        ps = _proj_rows(xs, norm_mix[l][None], w_in_r[l], bff[l], hconst[l], tok)

        conv_os = _conv(ps['u'], cache_conv[l], conv_w[l], conv_p3[l], row_off=0, b=bs, l=ls, tt=ls)

        flf_pt = pp['flf'][:, :H, :]
        flf_s = ps['flf'][:, :H].reshape(bs, ls, H)
        c_p = _cumsum_lanes(flf_pt.reshape(bp * H, sp), LOG2E).reshape(bp, H, sp)
        c_s = _row_cumsum(jnp.concatenate([cache_fox_logf[l].astype(F32), flf_s], axis=1), LANES, LOG2E)
        fox_op = _attention('fox', pp['fq'], pp['fk_tb'], pp['fv_tb'],
                            (_tiles(c_p, sp // tq_p, tq_p), jnp.swapaxes(c_p, 1, 2).reshape(tp, H)),
                            None, layer=l, b=bp, l=sp, tq=tq_p, tkp=tk_past)
        fox_os = _attention('fox', ps['fq'], ps['fk_b'], ps['fv_b'],
                            (_tiles(c_s[:, :, past:], 1, ls), jnp.swapaxes(c_s[:, :, past:], 1, 2).reshape(ts, H)),
                            (fkc, fvc, _tiles(c_s[:, :, :past], past // tk_past, tk_past)),
                            layer=l, b=bs, l=ls, tq=ls, tkp=tk_past)

        hg_os, st_s = _hgrn(ps['hq'], ps['hk'], ps['hv'], ps['hlf'], ps['hgate'], hn[l],
                            st0_all[l], row_off=0, b=bs, l=ls, tt=ls)

        sb_op = _attention('sb', pp['sq'], pp['sk_tb'], pp['sv_tb'], (), None,
                           layer=l, b=bp, l=sp, tq=tq_p, tkp=tk_past)
        sb_os = _attention('sb', ps['sq'], ps['sk_b'], ps['sv_b'], (), (skc, svc),
                           layer=l, b=bs, l=ls, tq=ls, tkp=tk_past)

        x1p, qmp = _c1(xp, [conv_op, fox_op, hg_op, sb_op], w_out_b[l], norm_mem[l][None], w_mq_b[l], tok)
        x1s, qms = _c1(xs, [conv_os, fox_os, hg_os, sb_os], w_out_b[l], norm_mem[l][None], w_mq_b[l], tok)

        mk, mkb, mv, mvb = _memkv(mem2, w_mk_b[l], w_mv_b[l], N_MEM)
        om_p = _memattn(qmp, mkb, mvb, row_off=0, b=bp, l=sp, tq=tok)
        om_s = _memattn(qms, cache_mem_k[l].reshape(bs * N_MEM, D_MODEL),
                        cache_mem_v[l].reshape(bs * N_MEM, D_MODEL), row_off=0, b=bs, l=ls, tq=ls)

        final = l == depth - 1
        mlp_w = (w_mo_b[l], norm_ffn[l][None], w_up_b[l], w_down_b[l])
        if final:
            xp = _c3(x1p, om_p, *mlp_w, norm_final[None], tt=tok, tf=1024, final=True)
        else:
            xp, pp = _mlp_proj(x1p, om_p, *mlp_w, *seq_params(l + 1), bufs, layer=l + 1, depth=depth,
                               b=bp, l=sp, tt=tt_fused, tile_w=tq_p, tf=1024)
        xs = _c3(x1s, om_s, w_mo_b[l], norm_ffn[l][None], w_up_b[l], w_down_b[l], norm_final[None],
                 tt=tok, tf=1024, final=final)

        u_s = jnp.concatenate([cache_conv[l].astype(F32), ps['u'].reshape(bs, ls, G)], axis=1)
        outs['conv_p'].append(u_p.reshape(bp, sp, G)[:, sp - HALO:])
        outs['conv_s'].append(u_s[:, ls:])
        for name in _PROJ_KV:
            outs[name + '_s'].append(ps[name])
        outs['flf_p'].append(flf_pt)
        outs['flf_s'].append(flf_s)
        outs['hg_p'].append(st_p)
        outs['hg_s'].append(st_s)
        outs['mk_p'].append(mk)
        outs['mv_p'].append(mv)

    st = lambda n: jnp.stack(outs[n])
    kv_p = [jnp.transpose(t.reshape(depth, bp, H, HD, sp), (0, 1, 4, 2, 3)) for t in bufs]
    kv_s = {n: st(n + '_s').reshape(depth, bs, ls, H, HD) for n in _PROJ_KV}
    mem_p = [st(n).reshape(depth, bp, N_MEM, MEM_HEADS, MEM_HD) for n in ('mk_p', 'mv_p')]
    hg_p = _unblock_diag_t(st('hg_p').reshape(depth * bp, G, G)).reshape(depth, bp, H, HD, HD)
    hg_s = _unblock_diag_t(st('hg_s').reshape(depth * bs, G, G)).reshape(depth, bs, H, HD, HD)
    return (xp.reshape(bp, sp, D_MODEL), xs.reshape(bs, ls, D_MODEL),
            st('conv_p'), kv_p[0], kv_p[1], jnp.swapaxes(st('flf_p'), 2, 3), hg_p,
            kv_p[2], kv_p[3], mem_p[0], mem_p[1],
            st('conv_s'), kv_s['fk'], kv_s['fv'], st('flf_s'), hg_s,
            kv_s['sk'], kv_s['sv'])
```

```python
import functools

import jax
import jax.numpy as jnp
from jax import lax
from jax.experimental import pallas as pl
from jax.experimental.pallas import tpu as pltpu

F32 = jnp.float32
BF16 = jnp.bfloat16

D_MODEL = 1024
G = 256
H = 4
HD = 64
CONV_W = 31
HALO = CONV_W - 1
N_MEM = 256
MEM_HEADS = 4
MEM_HD = 256
D_FF = 4096
EPS = 1e-6
NEG_BIG = -1e30
SUB = 8
CHUNK = 64
SUB_SHIFT = SUB.bit_length() - 1
CHUNK_SHIFT = CHUNK.bit_length() - 1
HD_SHIFT = HD.bit_length() - 1
FF_PAD = 128
LANES = 128
SB_BLOCK = 256
ATTN_GROUPS = 8
ATTN_GROUP_MIN_ROWS = 256
LOG2E = 1.4426950408889634
SOFTPLUS_LINEAR = 64.0
QK_SCALE = HD ** -0.5 * LOG2E
VMEM_LIMIT_BYTES = 52 * 1024 * 1024


def _params(*sem):
    return pltpu.CompilerParams(dimension_semantics=sem, vmem_limit_bytes=VMEM_LIMIT_BYTES)


def _logsig(x):
    return jnp.minimum(x, 0.0) - jnp.log(1.0 + jnp.exp(-jnp.abs(x)))


def _sigmoid(x):
    return 1.0 / (1.0 + jnp.exp(-x))


def _rms(x, g):
    return x * lax.rsqrt(jnp.mean(x * x, axis=-1, keepdims=True) + EPS) * g


def _dot(a, b):
    return jnp.dot(a, b, preferred_element_type=F32)


def _dot_nt(a, b):
    return lax.dot_general(a, b, (((1,), (1,)), ((), ())), preferred_element_type=F32)


def _dot_tn(a, b):
    return lax.dot_general(a, b, (((0,), (0,)), ((), ())), preferred_element_type=F32)


def _stack_heads(x, lane_head):
    return jnp.concatenate([jnp.where(lane_head == h, x, jnp.zeros_like(x)) for h in range(H)], axis=0)


def _unstack_heads(x4, lane_head, t):
    out = jnp.zeros((t, G), x4.dtype)
    for h in range(H):
        out = jnp.where(lane_head == h, x4[h * t:(h + 1) * t], out)
    return out


def _proj_groups(x_ref, g_ref, w_ref, bff_ref, hc_ref):
    h = _rms(x_ref[...], g_ref[...]).astype(BF16)

    def col(i, n=G):
        return _dot(h, w_ref[:, i * G:i * G + n])

    yield 'u', col(0) * _sigmoid(col(1))
    yield 'fq', col(2) * QK_SCALE
    yield 'fk', col(3)
    yield 'fv', col(4)
    yield 'hq', col(5)
    hz = col(6)
    base = hc_ref[0:1, :] + _logsig(hz)
    c1 = hc_ref[1:2, :]
    lae = jnp.maximum(c1, base) + jnp.log(1.0 + jnp.exp(-jnp.abs(c1 - base)))
    yield 'hlf', jnp.where(hc_ref[2:3, :] > 0.5, lae, base)
    yield 'hk', hc_ref[3:4, :] * _sigmoid(-hz)
    yield 'hv', col(7)
    yield 'hgate', col(8)
    yield 'sq', col(9) * QK_SCALE
    yield 'sk', col(10)
    yield 'sv', col(11)
    yield 'flf', _logsig(col(12, FF_PAD) + bff_ref[...])


_PROJ_ROW_F32 = ('u', 'hq', 'hlf', 'hk', 'hv', 'hgate')
_PROJ_ROW_BF16 = ('fq', 'sq')
_PROJ_KV = ('fk', 'fv', 'sk', 'sv')


def _proj_rows_kernel(x_ref, g_ref, w_ref, bff_ref, hc_ref, *out_refs):
    names = _PROJ_ROW_F32 + _PROJ_ROW_BF16 + ('flf',) + tuple(n + s for n in _PROJ_KV for s in ('', '_b'))
    refs = dict(zip(names, out_refs))
    for name, val in _proj_groups(x_ref, g_ref, w_ref, bff_ref, hc_ref):
        if name in _PROJ_KV:
            refs[name][...] = val
            refs[name + '_b'][...] = val.astype(BF16)
        else:
            refs[name][...] = val.astype(refs[name].dtype)


def _proj_rows(x, g, w, bff, hc, tt):
    t = x.shape[0]
    row = lambda i: (i, 0)
    fix = lambda i: (0, 0)
    f32o = jax.ShapeDtypeStruct((t, G), F32)
    b16o = jax.ShapeDtypeStruct((t, G), BF16)
    outs = ([f32o] * len(_PROJ_ROW_F32) + [b16o] * len(_PROJ_ROW_BF16)
            + [jax.ShapeDtypeStruct((t, FF_PAD), F32)] + [f32o, b16o] * len(_PROJ_KV))
    names = _PROJ_ROW_F32 + _PROJ_ROW_BF16 + ('flf',) + tuple(n + s for n in _PROJ_KV for s in ('', '_b'))
    res = pl.pallas_call(
        _proj_rows_kernel,
        grid=(t // tt,),
        in_specs=[pl.BlockSpec((tt, D_MODEL), row), pl.BlockSpec((1, D_MODEL), fix),
                  pl.BlockSpec((D_MODEL, w.shape[1]), fix), pl.BlockSpec((1, FF_PAD), fix),
                  pl.BlockSpec((4, G), fix)],
        out_specs=[pl.BlockSpec((tt, o.shape[1]), row) for o in outs],
        out_shape=outs,
        compiler_params=_params("parallel"),
        name="proj_rows",
    )(x, g, w, bff, hc)
    return dict(zip(names, res))


_SEQ_ROW_F32 = ('u',)
_SEQ_ROW_BF16 = ('fq', 'sq', 'conv_o', 'hg_o')
_SEQ_HGRN_IN = ('hq', 'hk', 'hv', 'hlf', 'hgate')
_SEQ_NAMES = (_SEQ_ROW_F32 + _SEQ_ROW_BF16 + ('flf',)
              + tuple(n + s for n in _PROJ_KV for s in ('_t', '_tb')) + ('st_out',))


def _proj_seq_kernel(x_ref, g_ref, w_ref, bff_ref, hc_ref, cw_ref, cp_ref, hn_ref, *rest, nt, tt, depth):
    refs = dict(zip(_SEQ_NAMES, rest[:len(_SEQ_NAMES)]))
    scratch = rest[len(_SEQ_NAMES):]
    i = pl.program_id(0) % nt
    _seq_start(i, scratch, tt)
    _seq_tile(x_ref, g_ref, w_ref, bff_ref, hc_ref, cw_ref, cp_ref, hn_ref, refs, scratch, tt, first_of=depth)
    _seq_finish(i, nt, refs, scratch)


def _seq_start(i, scratch, tt):
    ext_ref, st_ref = scratch[:2]

    @pl.when(i == 0)
    def _():
        _conv_start(ext_ref, jnp.zeros((HALO, G), F32), tt)
        st_ref[...] = jnp.zeros(st_ref.shape, F32)


def _seq_finish(i, nt, refs, scratch):
    @pl.when(i == nt - 1)
    def _():
        refs['st_out'][0] = scratch[1][...]


def _seq_tile(x_ref, g_ref, w_ref, bff_ref, hc_ref, cw_ref, cp_ref, hn_ref, refs, scratch, tt,
              between=lambda: None, first_of=None):
    ext_ref, st_ref, kpad, vpad, bpad = scratch
    vals = {}
    for name, val in _proj_groups(x_ref, g_ref, w_ref, bff_ref, hc_ref):
        if name in _PROJ_KV:
            vt = val.T
            refs[name + '_t'][0, 0] = vt
            for d in range(1, first_of or 1):
                refs[name + '_t'][d, 0] = jnp.zeros_like(vt)
            refs[name + '_tb'][0, 0] = vt.astype(BF16)
        elif name == 'flf':
            refs[name][0] = val.T[0:8, :]
        elif name in _SEQ_HGRN_IN:
            vals[name] = val
        else:
            refs[name][...] = val.astype(refs[name].dtype)
            if name == 'u':
                vals[name] = val
    between()
    _conv_tile(vals['u'], cw_ref, cp_ref, refs['conv_o'], ext_ref, tt)
    between()
    _hgrn_tile(vals['hq'], vals['hk'], vals['hv'], vals['hlf'], vals['hgate'], hn_ref,
               refs['hg_o'], st_ref, kpad, vpad, bpad, tt, between)


def _seq_outputs(t, depth, b, l, tt, tile_w, layer, tile_of):
    nt = l // tt
    per = tile_w // tt
    f32o = jax.ShapeDtypeStruct((t, G), F32)
    b16o = jax.ShapeDtypeStruct((t, G), BF16)
    stacked = jax.ShapeDtypeStruct((depth, b, G, l), F32)
    tiled = jax.ShapeDtypeStruct((b, l // tile_w, G, tile_w), BF16)
    n_rows = len(_SEQ_ROW_F32) + len(_SEQ_ROW_BF16)
    outs = ([f32o] * len(_SEQ_ROW_F32) + [b16o] * len(_SEQ_ROW_BF16)
            + [jax.ShapeDtypeStruct((b, 8, l), F32)] + [stacked, tiled] * len(_PROJ_KV)
            + [jax.ShapeDtypeStruct((b, G, G), F32)])
    seq = lambda i: tile_of(i) // nt
    pos = lambda i: tile_of(i) % nt
    stacked_spec = (pl.BlockSpec((depth, 1, G, tt), lambda i: (0, seq(i), 0, pos(i))) if layer is None else
                    pl.BlockSpec((1, 1, G, tt), lambda i: (layer, seq(i), 0, pos(i))))
    ospecs = ([pl.BlockSpec((tt, G), lambda i: (tile_of(i), 0))] * n_rows
              + [pl.BlockSpec((1, 8, tt), lambda i: (seq(i), 0, pos(i)))]
              + [stacked_spec,
                 pl.BlockSpec((1, 1, G, tt), lambda i: (seq(i), pos(i) // per, 0, pos(i) % per))] * len(_PROJ_KV)
              + [pl.BlockSpec((1, G, G), lambda i: (seq(i), 0, 0))])
    return outs, ospecs, n_rows + 1


def _seq_scratch(tt):
    return ([pltpu.VMEM((tt + 40, G), F32), pltpu.VMEM((G, G), F32)]
            + [pltpu.VMEM((tt + SUB, G), F32)] * 3)


def _seq_in_specs(w, resident):
    fix = lambda i: (0, 0)
    spec = (lambda shape: pl.BlockSpec(shape, fix, pipeline_mode=pl.Buffered(1))) if resident else \
           (lambda shape: pl.BlockSpec(shape, fix))
    return [pl.BlockSpec((1, D_MODEL), fix), spec((D_MODEL, w.shape[1])), pl.BlockSpec((1, FF_PAD), fix),
            pl.BlockSpec((4, G), fix), pl.BlockSpec((CONV_W, G), fix), pl.BlockSpec((3, G), fix),
            pl.BlockSpec((1, G), fix)]


def _proj_seq(x, g, w, bff, hc, cw, cp, hn, *, depth, b, l, tt):
    t = x.shape[0]
    outs, ospecs, _ = _seq_outputs(t, depth, b, l, tt, tt, None, lambda i: i)
    res = pl.pallas_call(
        functools.partial(_proj_seq_kernel, nt=l // tt, tt=tt, depth=depth),
        grid=(t // tt,),
        in_specs=[pl.BlockSpec((tt, D_MODEL), lambda i: (i, 0))] + _seq_in_specs(w, False),
        out_specs=ospecs,
        out_shape=outs,
        scratch_shapes=_seq_scratch(tt),
        compiler_params=_params("arbitrary"),
        name="proj_seq",
    )(x, g, w, bff, hc, cw, cp, hn)
    return dict(zip(_SEQ_NAMES, res))


def _mlp_proj_kernel(x1_ref, o_ref, wmo_ref, gf_ref, wup_ref, wdn_ref,
                     g_ref, w_ref, bff_ref, hc_ref, cw_ref, cp_ref, hn_ref, *rest, n_alias, nt, tt, tf):
    rest = rest[n_alias:]
    x3_ref = rest[0]
    refs = dict(zip(_SEQ_NAMES, rest[1:1 + len(_SEQ_NAMES)]))
    xprev_ref = rest[1 + len(_SEQ_NAMES)]
    scratch = rest[2 + len(_SEQ_NAMES):]
    step = pl.program_id(0)
    i = jnp.maximum(step - 1, 0) % nt

    @pl.when(step == 0)
    def _():
        xprev_ref[...] = jnp.zeros(xprev_ref.shape, F32)

    _seq_start(i, scratch, tt)
    mlp = {}
    pieces = _mlp_pieces(x1_ref, o_ref, wmo_ref, gf_ref, wup_ref, wdn_ref, tf, mlp)

    def between():
        if pieces:
            pieces.pop(0)()

    _seq_tile(xprev_ref, g_ref, w_ref, bff_ref, hc_ref, cw_ref, cp_ref, hn_ref, refs, scratch, tt, between)
    while pieces:
        between()
    x3_ref[...] = mlp['x3']
    xprev_ref[...] = mlp['x3']
    _seq_finish(i, nt, refs, scratch)


def _mlp_proj(x1, o, wmo, gf, wup, wdn, g, w, bff, hc, cw, cp, hn, bufs, *, layer, depth, b, l, tt, tile_w, tf):
    t = x1.shape[0]
    n_tiles = t // tt
    mlp_tile = lambda s: (jnp.minimum(s, n_tiles - 1), 0)
    fix = lambda s: (0, 0)
    resident = lambda shape: pl.BlockSpec(shape, fix, pipeline_mode=pl.Buffered(1))
    outs, ospecs, first_stacked = _seq_outputs(t, depth, b, l, tt, tile_w, layer, lambda s: jnp.maximum(s - 1, 0))
    n_in = 13
    alias_args = list(bufs)
    aliases = {n_in + k: 1 + first_stacked + 2 * k for k in range(len(_PROJ_KV))}
    res = pl.pallas_call(
        functools.partial(_mlp_proj_kernel, n_alias=len(alias_args), nt=l // tt, tt=tt, tf=tf),
        grid=(n_tiles + 1,),
        in_specs=[pl.BlockSpec((tt, D_MODEL), mlp_tile), pl.BlockSpec((tt, D_MODEL), mlp_tile),
                  resident((D_MODEL, D_MODEL)), pl.BlockSpec((1, D_MODEL), fix),
                  resident((D_MODEL, D_FF)), resident((D_FF, D_MODEL))] + _seq_in_specs(w, True)
                 + [pl.BlockSpec(memory_space=pl.ANY)] * len(alias_args),
        out_specs=[pl.BlockSpec((tt, D_MODEL), mlp_tile)] + ospecs,
        out_shape=[jax.ShapeDtypeStruct((t, D_MODEL), F32)] + outs,
        input_output_aliases=aliases,
        scratch_shapes=[pltpu.VMEM((tt, D_MODEL), F32)] + _seq_scratch(tt),
        compiler_params=_params("arbitrary"),
        name="mlp_proj",
    )(x1, o, wmo, gf, wup, wdn, g, w, bff, hc, cw, cp, hn, *alias_args)
    return res[0], dict(zip(_SEQ_NAMES, res[1:]))


def _cumsum_kernel(x_ref, o_ref, *, scale):
    x = x_ref[...]
    n = x.shape[1]
    lane = lax.broadcasted_iota(jnp.int32, x.shape, 1)
    sh = 1
    while sh < n:
        x = x + jnp.where(lane >= sh, pltpu.roll(x, sh, axis=1), 0.0)
        sh *= 2
    o_ref[...] = x * scale


def _cumsum_lanes(x, scale):
    return pl.pallas_call(
        functools.partial(_cumsum_kernel, scale=scale),
        out_shape=jax.ShapeDtypeStruct(x.shape, F32),
        compiler_params=pltpu.CompilerParams(vmem_limit_bytes=VMEM_LIMIT_BYTES),
        name="cumsum",
    )(x)


CONV_BASE = 32 - HALO


def _conv_start(ext_ref, hist, tt):
    ext_ref[0:CONV_BASE, :] = jnp.zeros((CONV_BASE, G), F32)
    ext_ref[CONV_BASE:32, :] = hist
    ext_ref[32 + tt:40 + tt, :] = jnp.zeros((8, G), F32)


def _conv_tile(u, w_ref, p_ref, o_ref, ext_ref, tt):
    base = CONV_BASE
    ext_ref[32:32 + tt, :] = u
    acc = None
    for r in range(8):
        part = None
        for m in range((base + CONV_W - 1) // 8 + 1):
            j = 8 * m + r - base
            if 0 <= j < CONV_W:
                term = w_ref[j:j + 1, :] * ext_ref[8 * m:8 * m + tt + 8, :]
                part = term if part is None else part + term
        part = part[r:r + tt, :]
        acc = part if acc is None else acc + part
    yf = acc + p_ref[0:1, :]
    mu = jnp.mean(yf, axis=-1, keepdims=True)
    d = yf - mu
    var = jnp.mean(d * d, axis=-1, keepdims=True)
    yn = d * lax.rsqrt(var + EPS) * p_ref[1:2, :] + p_ref[2:3, :]
    o_ref[...] = (yn * _sigmoid(yn)).astype(BF16)
    ext_ref[base:32, :] = ext_ref[tt + base:tt + 32, :]


def _conv_kernel(u_ref, hist_ref, w_ref, p_ref, o_ref, ext_ref, *, tt):
    @pl.when(pl.program_id(1) == 0)
    def _():
        _conv_start(ext_ref, hist_ref[0], tt)

    _conv_tile(u_ref[...], w_ref, p_ref, o_ref, ext_ref, tt)


def _conv(u, hist, w, p, *, row_off, b, l, tt):
    nt = l // tt
    off = row_off // tt
    return pl.pallas_call(
        functools.partial(_conv_kernel, tt=tt),
        grid=(b, nt),
        in_specs=[pl.BlockSpec((tt, G), lambda bi, i: (off + bi * nt + i, 0)),
                  pl.BlockSpec((1, HALO, G), lambda bi, i: (bi, 0, 0)),
                  pl.BlockSpec((CONV_W, G), lambda bi, i: (0, 0)),
                  pl.BlockSpec((3, G), lambda bi, i: (0, 0))],
        out_specs=pl.BlockSpec((tt, G), lambda bi, i: (bi * nt + i, 0)),
        out_shape=jax.ShapeDtypeStruct((b * l, G), BF16),
        scratch_shapes=[pltpu.VMEM((tt + 40, G), F32)],
        compiler_params=_params("arbitrary", "arbitrary"),
        name="conv",
    )(u, hist, w, p)


def _lanes(x, w):
    if w < LANES:
        return x[:, :w]
    return x if w == LANES else jnp.concatenate([x] * (w // LANES), axis=1)


def _scores(q4, k, t_layout):
    return _dot(q4, k) if t_layout else _dot_nt(q4, k)


def _weighted(p4, v, t_layout):
    return _dot_nt(p4, v) if t_layout else _dot(p4, v)


def _fox_kernel(*refs, tq, tkp, n_past, cur_t):
    if n_past:
        (q_ref, kc_ref, vc_ref, ckc_ref, cq_ref, kp_ref, vp_ref, ckp_ref,
         o_ref, m_ref, l_ref, acc_ref) = refs
    else:
        q_ref, kc_ref, vc_ref, ckc_ref, cq_ref, o_ref, m_ref, l_ref, acc_ref = refs
    i = pl.program_id(1)
    lane_head = lax.broadcasted_iota(jnp.int32, (tq, G), 1) >> HD_SHIFT
    q4 = _stack_heads(q_ref[...], lane_head)
    cq = cq_ref[...]
    cqb = [jnp.broadcast_to(cq[:, h:h + 1], (tq, LANES)) for h in range(H)]
    tri = (lax.broadcasted_iota(jnp.int32, (tq, tq), 1)
           <= lax.broadcasted_iota(jnp.int32, (tq, tq), 0))
    m_ref[...] = jnp.full(m_ref.shape, NEG_BIG, F32)
    l_ref[...] = jnp.zeros(l_ref.shape, F32)
    acc_ref[...] = jnp.zeros(acc_ref.shape, F32)

    def step(k, v, ck, masked, t_layout):
        s = _scores(q4, k, t_layout)
        w = s.shape[1]
        ps = []
        for h in range(H):
            sl = slice(h * tq, (h + 1) * tq)
            sh = s[sl] - ck[h:h + 1, :]
            if masked:
                sh = jnp.where(tri, sh, NEG_BIG)
            m_old = m_ref[sl]
            m_new = jnp.maximum(m_old, jnp.max(sh, axis=-1, keepdims=True) + cqb[h])
            p = jnp.exp2(sh - _lanes(m_new - cqb[h], w))
            alpha = jnp.exp2(m_old - m_new)
            l_ref[sl] = alpha * l_ref[sl] + jnp.sum(p, axis=-1, keepdims=True)
            m_ref[sl] = m_new
            acc_ref[sl] = acc_ref[sl] * _lanes(alpha, G)
            ps.append(p.astype(BF16))
        acc_ref[...] += _weighted(jnp.concatenate(ps, axis=0), v, t_layout)

    for j in range(n_past):
        sl = slice(j * tkp, (j + 1) * tkp)
        step(kp_ref[0, 0, :, sl].astype(BF16), vp_ref[0, 0, :, sl].astype(BF16), ckp_ref[0, j], False, True)

    if cur_t:
        def cur_body(j, c):
            step(kc_ref[0, j], vc_ref[0, j], ckc_ref[0, j], False, True)
            return c
        lax.fori_loop(0, i, cur_body, 0)
        step(kc_ref[0, i], vc_ref[0, i], ckc_ref[0, i], True, True)
    else:
        step(kc_ref[...], vc_ref[...], ckc_ref[0, 0], True, False)

    acc4 = jnp.concatenate(
        [acc_ref[h * tq:(h + 1) * tq] * _lanes(1.0 / l_ref[h * tq:(h + 1) * tq], G) for h in range(H)],
        axis=0)
    o_ref[...] = _unstack_heads(acc4, lane_head, tq).astype(BF16)


def _sb_kernel(*refs, tq, tkp, n_past, cur_t):
    if n_past:
        q_ref, kc_ref, vc_ref, kp_ref, vp_ref, o_ref, r_ref, acc_ref = refs
    else:
        q_ref, kc_ref, vc_ref, o_ref, r_ref, acc_ref = refs
    i = pl.program_id(1)
    lane_head = lax.broadcasted_iota(jnp.int32, (tq, G), 1) >> HD_SHIFT
    q4 = _stack_heads(q_ref[...], lane_head)
    r_ref[...] = jnp.zeros(r_ref.shape, F32)
    acc_ref[...] = jnp.zeros(acc_ref.shape, F32)

    def later_matrix(n):
        later = (lax.broadcasted_iota(jnp.int32, (n, n), 0) > lax.broadcasted_iota(jnp.int32, (n, n), 1))
        return jnp.where(later, 1.0, 0.0).astype(BF16)

    widths = {min(tq, SB_BLOCK)} | ({min(tkp, SB_BLOCK)} if n_past else set())
    later = {n: later_matrix(n) for n in widths}

    tri = (lax.broadcasted_iota(jnp.int32, (tq, tq), 1)
           < lax.broadcasted_iota(jnp.int32, (tq, tq), 0))
    valid = jnp.concatenate([tri] * H, axis=0)
    groups = ATTN_GROUPS if H * tq // ATTN_GROUPS >= ATTN_GROUP_MIN_ROWS else 1

    def step(k, v, masked, t_layout):
        for g in range(groups):
            rows = slice(g * (H * tq // groups), (g + 1) * (H * tq // groups))
            z = _scores(q4[rows], k, t_layout)
            w = z.shape[1]
            sp = jnp.where(z > SOFTPLUS_LINEAR, z, jnp.log2(1.0 + jnp.exp2(z)))
            u = jnp.where(valid[rows], sp, 0.0) if masked else sp
            bw = min(w, SB_BLOCK)
            r_run = r_ref[rows]
            rests = [None] * (w // bw)
            for blk in reversed(range(w // bw)):
                ub = u[:, blk * bw:(blk + 1) * bw]
                rests[blk] = _dot(ub.astype(BF16), later[bw]) + _lanes(r_run, bw)
                r_run = r_run + jnp.sum(ub, axis=-1, keepdims=True)
            rest = rests[0] if len(rests) == 1 else jnp.concatenate(rests, axis=1)
            a = jnp.exp2(z - sp - rest)
            if masked:
                a = jnp.where(valid[rows], a, 0.0)
            acc_ref[rows] += _weighted(a.astype(BF16), v, t_layout)
            r_ref[rows] = r_run

    if cur_t:
        step(kc_ref[0, i], vc_ref[0, i], True, True)

        def cur_body(n, c):
            j = i - 1 - n
            step(kc_ref[0, j], vc_ref[0, j], False, True)
            return c
        lax.fori_loop(0, i, cur_body, 0)
    else:
        step(kc_ref[...], vc_ref[...], True, False)

    for j in reversed(range(n_past)):
        sl = slice(j * tkp, (j + 1) * tkp)
        step(kp_ref[0, 0, :, sl].astype(BF16), vp_ref[0, 0, :, sl].astype(BF16), False, True)

    o_ref[...] = _unstack_heads(acc_ref[...], lane_head, tq).astype(BF16)


def _attention(kind, q, kc, vc, extra, past, *, layer, b, l, tq, tkp):
    nq = l // tq
    cur_t = kc.ndim == 4
    assert cur_t or nq == 1
    if cur_t:
        kv_spec = pl.BlockSpec((1, nq, G, tq), lambda bi, i: (bi, 0, 0, 0))
    else:
        kv_spec = pl.BlockSpec((l, G), lambda bi, i: (bi, 0))
    in_specs = [pl.BlockSpec((tq, G), lambda bi, i: (bi * nq + i, 0)), kv_spec, kv_spec]
    args = [q, kc, vc]
    if kind == 'fox':
        ckc, cq = extra
        in_specs += [pl.BlockSpec((1, nq, H, tq), lambda bi, i: (bi, 0, 0, 0)),
                     pl.BlockSpec((tq, H), lambda bi, i: (bi * nq + i, 0))]
        args += [ckc, cq]
    n_past = 0
    if past is not None:
        plen = past[0].shape[3]
        n_past = plen // tkp
        cache_spec = pl.BlockSpec((1, 1, G, plen), lambda bi, i: (layer, bi, 0, 0))
        in_specs += [cache_spec, cache_spec]
        args += [past[0], past[1]]
        if kind == 'fox':
            in_specs.append(pl.BlockSpec((1, n_past, H, tkp), lambda bi, i: (bi, 0, 0, 0)))
            args.append(past[2])
    stat = pltpu.VMEM((H * tq, LANES), F32)
    acc = pltpu.VMEM((H * tq, G), F32)
    body = _fox_kernel if kind == 'fox' else _sb_kernel
    return pl.pallas_call(
        functools.partial(body, tq=tq, tkp=tkp, n_past=n_past, cur_t=cur_t),
        grid=(b, nq),
        in_specs=in_specs,
        out_specs=pl.BlockSpec((tq, G), lambda bi, i: (bi * nq + i, 0)),
        out_shape=jax.ShapeDtypeStruct((b * l, G), BF16),
        scratch_shapes=[stat, stat, acc] if kind == 'fox' else [stat, acc],
        compiler_params=_params("parallel", "arbitrary"),
        name=kind,
    )(*args)


def _hgrn_tile(q, kk, v, lf, gate, hn_ref, o_ref, st_ref, kpad, vpad, bpad, tt, between=lambda: None):
    nc = tt // CHUNK
    row = lax.broadcasted_iota(jnp.int32, (tt, G), 0)
    r64 = row & (CHUNK - 1)
    rsub = row & (SUB - 1)
    lane_head = lax.broadcasted_iota(jnp.int32, (tt, G), 1) >> HD_SHIFT
    same_head = (lax.broadcasted_iota(jnp.int32, (G, G), 0) >> HD_SHIFT
                 == lax.broadcasted_iota(jnp.int32, (G, G), 1) >> HD_SHIFT)
    bd = jnp.where(same_head, 1.0, 0.0).astype(BF16)

    b = lf * LOG2E
    sh = 1
    while sh < CHUNK:
        b = b + jnp.where(r64 >= sh, pltpu.roll(b, sh, axis=0), 0.0)
        sh *= 2

    def chunk_row(r):
        return jnp.concatenate(
            [jnp.broadcast_to(b[c * CHUNK + r:c * CHUNK + r + 1, :], (CHUNK, G)) for c in range(nc)], axis=0)

    zpad = jnp.zeros((SUB, G), F32)
    kpad[0:SUB, :] = zpad
    vpad[0:SUB, :] = zpad
    bpad[0:SUB, :] = zpad
    kpad[SUB:SUB + tt, :] = kk
    vpad[SUB:SUB + tt, :] = v
    bpad[SUB:SUB + tt, :] = b
    o = jnp.zeros((tt, G), F32)
    for d in range(SUB):
        ks = kpad[SUB - d:SUB - d + tt, :]
        vs = vpad[SUB - d:SUB - d + tt, :]
        bs = bpad[SUB - d:SUB - d + tt, :]
        p = q * ks * jnp.exp2(jnp.minimum(b - bs, 0.0))
        p = jnp.where(rsub >= d, p, 0.0)
        o = o + _dot(p.astype(BF16), bd) * vs
        if d == SUB // 2 - 1:
            between()
    between()

    nsub = CHUNK // SUB
    refs_b = [chunk_row(SUB * n - 1) for n in range(1, nsub)]
    sub = r64 >> SUB_SHIFT
    rq = b
    for n, rb in enumerate(refs_b, start=1):
        rq = jnp.where(sub == n, rb, rq)
    qt = q * jnp.exp2(jnp.minimum(b - rq, 0.0))
    gr = SUB * nc
    lane_head_g = lax.broadcasted_iota(jnp.int32, (gr, G), 1) >> HD_SHIFT
    g_row = lax.broadcasted_iota(jnp.int32, (gr, tt), 0)
    g_col = lax.broadcasted_iota(jnp.int32, (gr, tt), 1)
    key_sub = jnp.where(g_row >> SUB_SHIFT == g_col >> CHUNK_SHIFT, (g_col & (CHUNK - 1)) >> SUB_SHIFT, nsub)
    key_sub4 = jnp.concatenate([key_sub] * H, axis=0)
    a_all = []
    for n, rb in enumerate(refs_b, start=1):
        qn = jnp.concatenate([qt[c * CHUNK + SUB * n:c * CHUNK + SUB * (n + 1)] for c in range(nc)], axis=0)
        q4n = _stack_heads(qn, lane_head_g).astype(BF16)
        kt = (kk * jnp.exp2(jnp.minimum(rb - b, 0.0))).astype(BF16)
        a_all.append(jnp.where(key_sub4 < n, _dot_nt(q4n, kt), 0.0).astype(BF16))
    vb = v.astype(BF16)
    o_all = _dot(jnp.concatenate(a_all, axis=0), vb)
    o_sub = [_unstack_heads(o_all[(n - 1) * H * gr:n * H * gr], lane_head_g, gr) for n in range(1, nsub)]
    pieces = []
    for c in range(nc):
        pieces.append(jnp.zeros((SUB, G), F32))
        pieces += [o_sub[n - 1][c * SUB:(c + 1) * SUB] for n in range(1, nsub)]
    o = o + jnp.concatenate(pieces, axis=0)

    between()

    blast = chunk_row(CHUNK - 1)
    qs = (q * jnp.exp2(b)).astype(BF16)
    kd = (kk * jnp.exp2(jnp.minimum(blast - b, 0.0))).astype(BF16)
    dec = jnp.exp2(blast)
    outs = []
    for c in range(nc):
        sl = slice(c * CHUNK, (c + 1) * CHUNK)
        st = st_ref[...]
        outs.append(_dot_nt(qs[sl], st.astype(BF16)))
        upd = _dot_tn(vb[sl], kd[sl])
        st_ref[...] = st * dec[c * CHUNK:c * CHUNK + 1, :] + jnp.where(same_head, upd, 0.0)
    o = o + jnp.concatenate(outs, axis=0)

    sq = o * o
    hi = sq.astype(BF16)
    lo = (sq - hi.astype(F32)).astype(BF16)
    ms = (_dot(hi, bd) + _dot(lo, bd)) * (1.0 / HD)
    o_ref[...] = (o * lax.rsqrt(ms + EPS) * hn_ref[...] * (gate * _sigmoid(gate))).astype(BF16)


def _hgrn_kernel(*refs, tt, has_init):
    if has_init:
        (q_ref, k_ref, v_ref, lf_ref, gate_ref, hn_ref, st0_ref,
         o_ref, sto_ref, st_ref, kpad, vpad, bpad) = refs
    else:
        (q_ref, k_ref, v_ref, lf_ref, gate_ref, hn_ref,
         o_ref, sto_ref, st_ref, kpad, vpad, bpad) = refs
    i = pl.program_id(1)

    @pl.when(i == 0)
    def _():
        if has_init:
            st_ref[...] = st0_ref[0]
        else:
            st_ref[...] = jnp.zeros(st_ref.shape, F32)

    _hgrn_tile(q_ref[...], k_ref[...], v_ref[...], lf_ref[...], gate_ref[...], hn_ref,
               o_ref, st_ref, kpad, vpad, bpad, tt)

    @pl.when(i == pl.num_programs(1) - 1)
    def _():
        sto_ref[0] = st_ref[...]


def _hgrn(q, k, v, lf, gate, hn, st0, *, row_off, b, l, tt):
    nt = l // tt
    off = row_off // tt
    tok = pl.BlockSpec((tt, G), lambda bi, i: (off + bi * nt + i, 0))
    in_specs = [tok, tok, tok, tok, tok, pl.BlockSpec((1, G), lambda bi, i: (0, 0))]
    args = [q, k, v, lf, gate, hn]
    if st0 is not None:
        in_specs.append(pl.BlockSpec((1, G, G), lambda bi, i: (bi, 0, 0)))
        args.append(st0)
    return pl.pallas_call(
        functools.partial(_hgrn_kernel, tt=tt, has_init=st0 is not None),
        grid=(b, nt),
        in_specs=in_specs,
        out_specs=[pl.BlockSpec((tt, G), lambda bi, i: (bi * nt + i, 0)),
                   pl.BlockSpec((1, G, G), lambda bi, i: (bi, 0, 0))],
        out_shape=[jax.ShapeDtypeStruct((b * l, G), BF16), jax.ShapeDtypeStruct((b, G, G), F32)],
        scratch_shapes=[pltpu.VMEM((G, G), F32)] + [pltpu.VMEM((tt + SUB, G), F32)] * 3,
        compiler_params=_params("arbitrary", "arbitrary"),
        name="hgrn",
    )(*args)


def _c1_kernel(x_ref, m0_ref, m1_ref, m2_ref, m3_ref, wo_ref, g_ref, wq_ref, x1_ref, q_ref):
    acc = x_ref[...]
    for p, m_ref in enumerate((m0_ref, m1_ref, m2_ref, m3_ref)):
        acc = acc + _dot(m_ref[...], wo_ref[p * G:(p + 1) * G, :])
    x1_ref[...] = acc
    h = _rms(acc, g_ref[...]).astype(BF16)
    q_ref[...] = (_dot(h, wq_ref[...]) * (MEM_HD ** -0.5)).astype(BF16)


def _c1(x, mix, wo, g, wq, tt):
    t = x.shape[0]
    row = lambda i: (i, 0)
    fix = lambda i: (0, 0)
    return pl.pallas_call(
        _c1_kernel,
        grid=(t // tt,),
        in_specs=[pl.BlockSpec((tt, D_MODEL), row)] + [pl.BlockSpec((tt, G), row)] * 4
                 + [pl.BlockSpec((D_MODEL, D_MODEL), fix), pl.BlockSpec((1, D_MODEL), fix),
                    pl.BlockSpec((D_MODEL, D_MODEL), fix)],
        out_specs=[pl.BlockSpec((tt, D_MODEL), row), pl.BlockSpec((tt, D_MODEL), row)],
        out_shape=[jax.ShapeDtypeStruct((t, D_MODEL), F32), jax.ShapeDtypeStruct((t, D_MODEL), BF16)],
        compiler_params=_params("parallel"),
        name="outproj_memq",
    )(x, *mix, wo, g, wq)


def _memkv_kernel(m_ref, wk_ref, wv_ref, k_ref, kb_ref, v_ref, vb_ref):
    m = m_ref[...].astype(BF16)
    k = _dot(m, wk_ref[...])
    k_ref[...] = k
    kb_ref[...] = k.astype(BF16)
    v = _dot(m, wv_ref[...])
    v_ref[...] = v
    vb_ref[...] = v.astype(BF16)


def _memkv(mem, wk, wv, tt):
    t = mem.shape[0]
    row = lambda i: (i, 0)
    fix = lambda i: (0, 0)
    f32o = jax.ShapeDtypeStruct((t, D_MODEL), F32)
    b16o = jax.ShapeDtypeStruct((t, D_MODEL), BF16)
    return pl.pallas_call(
        _memkv_kernel,
        grid=(t // tt,),
        in_specs=[pl.BlockSpec((tt, D_MODEL), row), pl.BlockSpec((D_MODEL, D_MODEL), fix),
                  pl.BlockSpec((D_MODEL, D_MODEL), fix)],
        out_specs=[pl.BlockSpec((tt, D_MODEL), row)] * 4,
        out_shape=[f32o, b16o, f32o, b16o],
        compiler_params=_params("parallel"),
        name="memkv",
    )(mem, wk, wv)


def _memattn_tile(q, k_ref, v_ref):
    outs = []
    for h in range(MEM_HEADS):
        sl = slice(h * MEM_HD, (h + 1) * MEM_HD)
        s = _dot_nt(q[:, sl], k_ref[:, sl].astype(BF16))
        p = jnp.exp(s - jnp.max(s, axis=-1, keepdims=True))
        den = jnp.sum(p, axis=-1, keepdims=True)
        outs.append(_dot(p.astype(BF16), v_ref[:, sl].astype(BF16)) * (1.0 / den))
    return jnp.concatenate(outs, axis=-1).astype(BF16)


def _memattn_kernel(q_ref, k_ref, v_ref, o_ref):
    o_ref[...] = _memattn_tile(q_ref[...], k_ref, v_ref)


def _c1_memattn_kernel(x_ref, m0_ref, m1_ref, m2_ref, m3_ref, wo_ref, g_ref, wq_ref, k_ref, v_ref,
                       x1_ref, o_ref):
    acc = x_ref[...]
    for p, m_ref in enumerate((m0_ref, m1_ref, m2_ref, m3_ref)):
        acc = acc + _dot(m_ref[...], wo_ref[p * G:(p + 1) * G, :])
    x1_ref[...] = acc
    h = _rms(acc, g_ref[...]).astype(BF16)
    q = (_dot(h, wq_ref[...]) * (MEM_HD ** -0.5)).astype(BF16)
    o_ref[...] = _memattn_tile(q, k_ref, v_ref)


def _c1_memattn(x, mix, wo, g, wq, k, v, *, l, tt):
    t = x.shape[0]
    nt = l // tt
    row = lambda i: (i, 0)
    fix = lambda i: (0, 0)
    mem = pl.BlockSpec((N_MEM, D_MODEL), lambda i: (i // nt, 0))
    return pl.pallas_call(
        _c1_memattn_kernel,
        grid=(t // tt,),
        in_specs=[pl.BlockSpec((tt, D_MODEL), row)] + [pl.BlockSpec((tt, G), row)] * 4
                 + [pl.BlockSpec((D_MODEL, D_MODEL), fix), pl.BlockSpec((1, D_MODEL), fix),
                    pl.BlockSpec((D_MODEL, D_MODEL), fix), mem, mem],
        out_specs=[pl.BlockSpec((tt, D_MODEL), row), pl.BlockSpec((tt, D_MODEL), row)],
        out_shape=[jax.ShapeDtypeStruct((t, D_MODEL), F32), jax.ShapeDtypeStruct((t, D_MODEL), BF16)],
        compiler_params=_params("parallel"),
        name="outproj_memattn",
    )(x, *mix, wo, g, wq, k, v)


def _memattn(q, k, v, *, row_off, b, l, tq):
    nq = l // tq
    off = row_off // tq
    return pl.pallas_call(
        _memattn_kernel,
        grid=(b, nq),
        in_specs=[pl.BlockSpec((tq, D_MODEL), lambda bi, i: (off + bi * nq + i, 0)),
                  pl.BlockSpec((N_MEM, D_MODEL), lambda bi, i: (bi, 0)),
                  pl.BlockSpec((N_MEM, D_MODEL), lambda bi, i: (bi, 0))],
        out_specs=pl.BlockSpec((tq, D_MODEL), lambda bi, i: (bi * nq + i, 0)),
        out_shape=jax.ShapeDtypeStruct((b * l, D_MODEL), BF16),
        compiler_params=_params("parallel", "parallel"),
        name="memattn",
    )(q, k, v)


def _mlp_pieces(x1_ref, o_ref, wmo_ref, g_ref, wup_ref, wdn_ref, tf, out):
    def first():
        x2 = x1_ref[...] + _dot(o_ref[...], wmo_ref[...])
        out['h'] = _rms(x2, g_ref[...]).astype(BF16)
        out['x3'] = x2

    def chunk(f):
        a = jnp.maximum(_dot(out['h'], wup_ref[:, f * tf:(f + 1) * tf]), 0.0)
        out['x3'] = out['x3'] + _dot((a * a).astype(BF16), wdn_ref[f * tf:(f + 1) * tf, :])

    return [first] + [functools.partial(chunk, f) for f in range(D_FF // tf)]


def _mlp_tile(x1_ref, o_ref, wmo_ref, g_ref, wup_ref, wdn_ref, tf):
    out = {}
    for piece in _mlp_pieces(x1_ref, o_ref, wmo_ref, g_ref, wup_ref, wdn_ref, tf, out):
        piece()
    return out['x3']


def _c3_kernel(x1_ref, o_ref, wmo_ref, g_ref, wup_ref, wdn_ref, gf_ref, y_ref, *, final, tf):
    x3 = _mlp_tile(x1_ref, o_ref, wmo_ref, g_ref, wup_ref, wdn_ref, tf)
    y_ref[...] = _rms(x3, gf_ref[...]) if final else x3


def _c3(x1, o, wmo, g, wup, wdn, gf, *, tt, tf, final):
    t = x1.shape[0]
    row = lambda i: (i, 0)
    fix = lambda i: (0, 0)
    resident = lambda shape: pl.BlockSpec(shape, fix, pipeline_mode=pl.Buffered(1))
    return pl.pallas_call(
        functools.partial(_c3_kernel, final=final, tf=tf),
        grid=(t // tt,),
        in_specs=[pl.BlockSpec((tt, D_MODEL), row), pl.BlockSpec((tt, D_MODEL), row),
                  resident((D_MODEL, D_MODEL)), pl.BlockSpec((1, D_MODEL), fix),
                  resident((D_MODEL, D_FF)), resident((D_FF, D_MODEL)),
                  pl.BlockSpec((1, D_MODEL), fix)],
        out_specs=pl.BlockSpec((tt, D_MODEL), row),
        out_shape=jax.ShapeDtypeStruct((t, D_MODEL), F32),
        compiler_params=_params("parallel"),
        name="memout_mlp",
    )(x1, o, wmo, g, wup, wdn, gf)


def _block_diag_t(s):
    b = s.shape[0]
    eye = jnp.eye(H, dtype=s.dtype)
    st = jnp.swapaxes(s, 2, 3)
    return jnp.einsum('bhvk,hg->bhvgk', st, eye).reshape(b, G, G)


def _unblock_diag_t(st):
    b = st.shape[0]
    s5 = st.reshape(b, H, HD, H, HD)
    d = jnp.stack([s5[:, h, :, h, :] for h in range(H)], axis=1)
    return jnp.swapaxes(d, 2, 3)


def _row_cumsum(lf, tile, scale):
    b, s, _ = lf.shape
    sp = -(-s // tile) * tile
    x = jnp.swapaxes(lf, 1, 2).reshape(b * H, s)
    x = jnp.pad(x, ((0, 0), (0, sp - s)))
    return _cumsum_lanes(x, scale).reshape(b, H, sp)[:, :, :s]


def _tiles(c, n, t):
    b = c.shape[0]
    return jnp.swapaxes(c.reshape(b, H, n, t), 1, 2)


def kernel(x_prompt, x_sample, mem_prompt, cache_conv, cache_fox_k, cache_fox_v, cache_fox_logf,
           state_hgrn, cache_sb_k, cache_sb_v, cache_mem_k, cache_mem_v, norm_mix, w_in, b_fox_f,
           conv_w, conv_b, conv_ln_g, conv_ln_b, hgrn_lb, hgrn_norm, w_out, norm_mem, w_mq, w_mk,
           w_mv, w_mo, norm_ffn, w_up, w_down, norm_final):
    bp, sp, _ = x_prompt.shape
    bs, ls, _ = x_sample.shape
    depth = w_in.shape[0]
    past = cache_fox_k.shape[2]
    tp = bp * sp
    ts = bs * ls
    tok = 512
    tq_p = 512
    tk_past = 512
    tt_fused = 256

    sm = jax.nn.softmax(hgrn_lb.astype(F32), axis=0)
    lower = jnp.clip(jnp.cumsum(sm, axis=0) - sm[0], 0.0, 1.0 - 1e-6)
    pos = lower > 0.0
    hconst = jnp.stack([jnp.log1p(-lower), jnp.log(jnp.where(pos, lower, 1.0)),
                        pos.astype(F32), 1.0 - lower], axis=1)

    o_ff = 5 * G
    w_in_r = jnp.concatenate(
        [w_in[:, :, :o_ff], w_in[:, :, o_ff + H:],
         jnp.pad(w_in[:, :, o_ff:o_ff + H], ((0, 0), (0, 0), (0, FF_PAD - H)))], axis=-1).astype(BF16)
    bff = jnp.pad(b_fox_f, ((0, 0), (0, FF_PAD - H)))[:, None, :]
    w_out_b = w_out.astype(BF16)
    w_mq_b = w_mq.astype(BF16)
    w_mk_b = w_mk.astype(BF16)
    w_mv_b = w_mv.astype(BF16)
    w_mo_b = w_mo.astype(BF16)
    w_up_b = w_up.astype(BF16)
    w_down_b = w_down.astype(BF16)
    hn = jnp.tile(hgrn_norm, (1, H))[:, None, :]
    conv_p3 = jnp.stack([conv_b, conv_ln_g, conv_ln_b], axis=1)

    def seq_last(c):
        return jnp.transpose(c, (0, 1, 3, 4, 2)).reshape(depth, bs, G, past)

    fkc, fvc, skc, svc = (seq_last(c) for c in (cache_fox_k, cache_fox_v, cache_sb_k, cache_sb_v))

    st0_all = _block_diag_t(state_hgrn.astype(F32).reshape(depth * bs, H, HD, HD)).reshape(depth, bs, G, G)
    xp = x_prompt.reshape(tp, D_MODEL)
    xs = x_sample.reshape(ts, D_MODEL)
    mem2 = mem_prompt.reshape(bp * N_MEM, D_MODEL)

    outs = {n: [] for n in ('conv_p', 'flf_p', 'hg_p', 'mk_p', 'mv_p',
                            'conv_s', 'fk_s', 'fv_s', 'flf_s', 'hg_s', 'sk_s', 'sv_s')}
    def seq_params(l):
        return (norm_mix[l][None], w_in_r[l], bff[l], hconst[l], conv_w[l], conv_p3[l], hn[l])

    pp = _proj_seq(xp, *seq_params(0), depth=depth, b=bp, l=sp, tt=tq_p)
    for l in range(depth):
        bufs = tuple(pp[n + '_t'] for n in _PROJ_KV)
        conv_op, hg_op, st_p = pp['conv_o'], pp['hg_o'], pp['st_out']
        u_p = pp['u']
        ps = _proj_rows(xs, norm_mix[l][None], w_in_r[l], bff[l], hconst[l], tok)

        conv_os = _conv(ps['u'], cache_conv[l], conv_w[l], conv_p3[l], row_off=0, b=bs, l=ls, tt=ls)

        flf_pt = pp['flf'][:, :H, :]
        flf_s = ps['flf'][:, :H].reshape(bs, ls, H)
        c_p = _cumsum_lanes(flf_pt.reshape(bp * H, sp), LOG2E).reshape(bp, H, sp)
        c_s = _row_cumsum(jnp.concatenate([cache_fox_logf[l].astype(F32), flf_s], axis=1), LANES, LOG2E)
        fox_op = _attention('fox', pp['fq'], pp['fk_tb'], pp['fv_tb'],
                            (_tiles(c_p, sp // tq_p, tq_p), jnp.swapaxes(c_p, 1, 2).reshape(tp, H)),
                            None, layer=l, b=bp, l=sp, tq=tq_p, tkp=tk_past)
        fox_os = _attention('fox', ps['fq'], ps['fk_b'], ps['fv_b'],
                            (_tiles(c_s[:, :, past:], 1, ls), jnp.swapaxes(c_s[:, :, past:], 1, 2).reshape(ts, H)),
                            (fkc, fvc, _tiles(c_s[:, :, :past], past // tk_past, tk_past)),
                            layer=l, b=bs, l=ls, tq=ls, tkp=tk_past)

        hg_os, st_s = _hgrn(ps['hq'], ps['hk'], ps['hv'], ps['hlf'], ps['hgate'], hn[l],
                            st0_all[l], row_off=0, b=bs, l=ls, tt=ls)

        sb_op = _attention('sb', pp['sq'], pp['sk_tb'], pp['sv_tb'], (), None,
                           layer=l, b=bp, l=sp, tq=tq_p, tkp=tk_past)
        sb_os = _attention('sb', ps['sq'], ps['sk_b'], ps['sv_b'], (), (skc, svc),
                           layer=l, b=bs, l=ls, tq=ls, tkp=tk_past)

        mk, mkb, mv, mvb = _memkv(mem2, w_mk_b[l], w_mv_b[l], N_MEM)
        x1p, om_p = _c1_memattn(xp, [conv_op, fox_op, hg_op, sb_op], w_out_b[l], norm_mem[l][None], w_mq_b[l],
                                mkb, mvb, l=sp, tt=tok)
        x1s, qms = _c1(xs, [conv_os, fox_os, hg_os, sb_os], w_out_b[l], norm_mem[l][None], w_mq_b[l], tok)
        om_s = _memattn(qms, cache_mem_k[l].reshape(bs * N_MEM, D_MODEL),
                        cache_mem_v[l].reshape(bs * N_MEM, D_MODEL), row_off=0, b=bs, l=ls, tq=ls)

        final = l == depth - 1
        mlp_w = (w_mo_b[l], norm_ffn[l][None], w_up_b[l], w_down_b[l])
        if final:
            xp = _c3(x1p, om_p, *mlp_w, norm_final[None], tt=tok, tf=1024, final=True)
        else:
            xp, pp = _mlp_proj(x1p, om_p, *mlp_w, *seq_params(l + 1), bufs, layer=l + 1, depth=depth,
                               b=bp, l=sp, tt=tt_fused, tile_w=tq_p, tf=1024)
        xs = _c3(x1s, om_s, w_mo_b[l], norm_ffn[l][None], w_up_b[l], w_down_b[l], norm_final[None],
                 tt=tok, tf=1024, final=final)

        u_s = jnp.concatenate([cache_conv[l].astype(F32), ps['u'].reshape(bs, ls, G)], axis=1)
        outs['conv_p'].append(u_p.reshape(bp, sp, G)[:, sp - HALO:])
        outs['conv_s'].append(u_s[:, ls:])
        for name in _PROJ_KV:
            outs[name + '_s'].append(ps[name])
        outs['flf_p'].append(flf_pt)
        outs['flf_s'].append(flf_s)
        outs['hg_p'].append(st_p)
        outs['hg_s'].append(st_s)
        outs['mk_p'].append(mk)
        outs['mv_p'].append(mv)

    st = lambda n: jnp.stack(outs[n])
    kv_p = [jnp.transpose(t.reshape(depth, bp, H, HD, sp), (0, 1, 4, 2, 3)) for t in bufs]
    kv_s = {n: st(n + '_s').reshape(depth, bs, ls, H, HD) for n in _PROJ_KV}
    mem_p = [st(n).reshape(depth, bp, N_MEM, MEM_HEADS, MEM_HD) for n in ('mk_p', 'mv_p')]
    hg_p = _unblock_diag_t(st('hg_p').reshape(depth * bp, G, G)).reshape(depth, bp, H, HD, HD)
    hg_s = _unblock_diag_t(st('hg_s').reshape(depth * bs, G, G)).reshape(depth, bs, H, HD, HD)
    return (xp.reshape(bp, sp, D_MODEL), xs.reshape(bs, ls, D_MODEL),
            st('conv_p'), kv_p[0], kv_p[1], jnp.swapaxes(st('flf_p'), 2, 3), hg_p,
            kv_p[2], kv_p[3], mem_p[0], mem_p[1],
            st('conv_s'), kv_s['fk'], kv_s['fv'], st('flf_s'), hg_s,
            kv_s['sk'], kv_s['sv'])
```
